```python
import math, functools
import jax, jax.numpy as jnp
from jax import lax
import numpy as np

D_MODEL = 1024
BATCH = 4
SEQ = 4096
DEPTH = 2
DEC_BATCH = 128
DEC_SEQ = 1
PAST_LEN = 2048
PAGE_SIZE = 128

SC_DIM = D_MODEL
SC_WIDTH = 3
SSM_DIM = 2 * D_MODEL
SSM_HEAD_DIM = 64
SSM_HEADS = SSM_DIM // SSM_HEAD_DIM
SSM_GROUPS = 4
SSM_STATE = 128
SSM_GN = SSM_GROUPS * SSM_STATE
SSM_CONV = 4
SSM_CONV_DIM = SSM_DIM + 2 * SSM_GN
SSM_CHUNK = 128
SB_HEADS = 16
SB_HEAD_DIM = 64
SB_DIM = SB_HEADS * SB_HEAD_DIM
SB_BLOCK = 128
SB_SCALE = SB_HEAD_DIM ** -0.5
SB_BIAS_INIT = -6.0
N_MEM = 256
X_HEADS = 4
X_HEAD_DIM = D_MODEL // X_HEADS
X_SCALE = X_HEAD_DIM ** -0.5
D_FF = 4 * D_MODEL
N_BRANCH = 3
N_IN = 3 * SC_DIM + SSM_DIM + SSM_CONV_DIM + SSM_HEADS + 3 * SB_DIM + N_BRANCH * D_MODEL
EPS = 1e-6

kernel_name = "hybrid_conv_ssd_stickbreak_decoder_step"


def rmsnorm(x, g):
    xf = x.astype(jnp.float32)
    y = xf * lax.rsqrt(jnp.mean(xf * xf, axis=-1, keepdims=True) + EPS)
    return y.astype(x.dtype) * g


def in_split_points():
    sizes = (SC_DIM, SC_DIM, SC_DIM, SSM_DIM, SSM_CONV_DIM, SSM_HEADS, SB_DIM, SB_DIM, SB_DIM)
    return [sum(sizes[:i + 1]) for i in range(len(sizes))]


def causal_dwconv(x_pad, w):
    width = w.shape[0]
    t = x_pad.shape[1] - (width - 1)
    return sum(x_pad[:, j:j + t] * w[j] for j in range(width))


def segsum_exp(a_cs):
    l = a_cs.shape[-1]
    diff = a_cs[..., :, None] - a_cs[..., None, :]
    mask = jnp.tril(jnp.ones((l, l), dtype=bool))
    return jnp.exp(jnp.where(mask, diff, -jnp.inf))


def ssd(x, a, b, c, init_state, chunk):
    bt, t, h, p = x.shape
    g, n = b.shape[2], b.shape[3]
    r = h // g
    nc = t // chunk
    f32 = jnp.float32
    x = x.astype(f32).reshape(bt, nc, chunk, g, r, p)
    b = b.astype(f32).reshape(bt, nc, chunk, g, n)
    c = c.astype(f32).reshape(bt, nc, chunk, g, n)
    a = a.astype(f32).reshape(bt, nc, chunk, g, r).transpose(0, 3, 4, 1, 2)
    a_cs = jnp.cumsum(a, axis=-1)
    cb = jnp.einsum("bclgn,bcsgn->bgcls", c, b)
    y_diag = jnp.einsum("bgcls,bgrcls,bcsgrp->bclgrp", cb, segsum_exp(a_cs), x)
    decay_to_end = jnp.exp(a_cs[..., -1:] - a_cs)
    chunk_states = jnp.einsum("bcsgn,bgrcs,bcsgrp->cbgrpn", b, decay_to_end, x)
    chunk_decay = jnp.exp(a_cs[..., -1]).transpose(3, 0, 1, 2)
    s0 = init_state.astype(f32).reshape(bt, g, r, p, n)

    def step(s, inp):
        st, dc = inp
        return s * dc[..., None, None] + st, s

    s_final, prev = lax.scan(step, s0, (chunk_states, chunk_decay))
    y_off = jnp.einsum("bclgn,cbgrpn,bgrcl->bclgrp", c, prev, jnp.exp(a_cs))
    y = (y_diag + y_off).reshape(bt, t, h, p)
    return y, s_final.reshape(bt, h, p, n).astype(init_state.dtype)


def sb_weights(logits, mask):
    log_1m = jnp.where(mask, jax.nn.log_sigmoid(-logits), 0.0)
    later = lax.cumsum(log_1m, axis=logits.ndim - 1, reverse=True) - log_1m
    return jnp.where(mask, jnp.exp(jax.nn.log_sigmoid(logits) + later), 0.0)


def sb_prompt_attend(q, k, v, bias):
    b, t, h, d = q.shape
    nb = t // SB_BLOCK
    k_pos = jnp.arange(t)
    q_blocks = q.reshape(b, nb, SB_BLOCK, h, d).swapaxes(0, 1)
    p_blocks = k_pos.reshape(nb, SB_BLOCK)
    bias_f = bias.astype(jnp.float32)[:, None, None]

    def block(args):
        qb, qp = args
        logits = jnp.einsum("bqhd,bkhd->bhqk", qb, k).astype(jnp.float32) * SB_SCALE + bias_f
        w = sb_weights(logits, k_pos[None, :] < qp[:, None])
        return jnp.einsum("bhqk,bkhd->bqhd", w.astype(v.dtype), v)

    out = lax.map(block, (q_blocks, p_blocks))
    return out.swapaxes(0, 1).reshape(b, t, h, d)


def sb_sample_attend(q, k_new, v_new, bias, cache_k, cache_v, layer, page_table):
    n, t = q.shape[0], q.shape[1]
    k_past = cache_k[layer][page_table].reshape(n, -1, SB_HEADS, SB_HEAD_DIM)
    v_past = cache_v[layer][page_table].reshape(n, -1, SB_HEADS, SB_HEAD_DIM)
    past = k_past.shape[1]
    logits = jnp.concatenate([jnp.einsum("bqhd,bkhd->bhqk", q, k_past),
                              jnp.einsum("bqhd,bkhd->bhqk", q, k_new)], axis=-1)
    logits = logits.astype(jnp.float32) * SB_SCALE + bias.astype(jnp.float32)[:, None, None]
    q_pos = past + jnp.arange(t)
    k_pos = jnp.arange(past + t)
    w = sb_weights(logits, k_pos[None, :] < q_pos[:, None]).astype(v_new.dtype)
    return (jnp.einsum("bhqk,bkhd->bqhd", w[..., :past], v_past)
            + jnp.einsum("bhqk,bkhd->bqhd", w[..., past:], v_new))


def mem_attend(q, k, v):
    s = jnp.einsum("bqhd,bkhd->bhqk", q, k).astype(jnp.float32) * X_SCALE
    return jnp.einsum("bhqk,bkhd->bqhd", jax.nn.softmax(s, axis=-1).astype(v.dtype), v)


def trunk_layer(x, p, sc_buf, ssm_buf, ssm_state, chunk, attend, mem_k, mem_v):
    b, t, _ = x.shape
    hn = rmsnorm(x, p["g_mix"])
    (sc_b, sc_c, sc_h, z, xbc, dt, q, k, v, gates) = jnp.split(hn @ p["w_in"], in_split_points(), axis=-1)
    u_pad = jnp.concatenate([sc_buf.astype(x.dtype), sc_c * sc_h], axis=1)
    y_a = sc_b * causal_dwconv(u_pad, p["w_sc_conv"])
    xbc_pad = jnp.concatenate([ssm_buf.astype(x.dtype), xbc], axis=1)
    xbc_c = jax.nn.silu(causal_dwconv(xbc_pad, p["w_ssm_conv"]) + p["b_ssm_conv"])
    xs = xbc_c[..., :SSM_DIM].reshape(b, t, SSM_HEADS, SSM_HEAD_DIM)
    bs = xbc_c[..., SSM_DIM:SSM_DIM + SSM_GN].reshape(b, t, SSM_GROUPS, SSM_STATE)
    cs = xbc_c[..., SSM_DIM + SSM_GN:].reshape(b, t, SSM_GROUPS, SSM_STATE)
    dt = jax.nn.softplus(dt.astype(jnp.float32) + p["dt_bias"].astype(jnp.float32))
    a = -jnp.exp(p["a_log"].astype(jnp.float32))
    y, new_state = ssd(xs * dt[..., None], a * dt, bs, cs, ssm_state, chunk)
    y = (y + p["d_skip"].astype(jnp.float32)[:, None] * xs.astype(jnp.float32)).astype(x.dtype)
    y = (y.reshape(b, t, SSM_DIM) * jax.nn.silu(z)).reshape(b, t, SSM_GROUPS, SSM_DIM // SSM_GROUPS)
    y_b = rmsnorm(y, p["g_ssm_norm"].reshape(SSM_GROUPS, -1)).reshape(b, t, SSM_DIM)
    k4 = k.reshape(b, t, SB_HEADS, SB_HEAD_DIM)
    v4 = v.reshape(b, t, SB_HEADS, SB_HEAD_DIM)
    y_c = attend(q.reshape(b, t, SB_HEADS, SB_HEAD_DIM), k4, v4, p["sb_bias"]).reshape(b, t, SB_DIM)
    g = jax.nn.sigmoid(gates).reshape(b, t, N_BRANCH, D_MODEL)
    merged = (g[:, :, 0] * (y_a @ p["w_br_sc"]) + g[:, :, 1] * (y_b @ p["w_br_ssm"])
              + g[:, :, 2] * (y_c @ p["w_br_sb"]))
    x = x + merged @ p["w_out"]
    hn = rmsnorm(x, p["g_x"])
    qx = (hn @ p["w_xq"]).reshape(b, t, X_HEADS, X_HEAD_DIM)
    x = x + mem_attend(qx, mem_k, mem_v).reshape(b, t, D_MODEL) @ p["w_xo"]
    hn = rmsnorm(x, p["g_mlp"])
    x = x + jnp.square(jax.nn.relu(hn @ p["w_ff1"])) @ p["w_ff2"]
    return x, u_pad[:, -(SC_WIDTH - 1):], xbc_pad[:, -(SSM_CONV - 1):], new_state, k4, v4


def setup_inputs(seed: int = 0) -> dict:
    key = jax.random.key(seed)
    ks = iter(jax.random.split(key, 48))

    def nrm(shape, scale):
        return scale * jax.random.normal(next(ks), shape, jnp.float32)

    def gain(shape):
        return 1.0 + nrm(shape, 0.02)

    n_pages = PAST_LEN // PAGE_SIZE
    n_used = DEC_BATCH * n_pages
    n_pool = n_used + n_used // 4
    perm = jax.random.permutation(next(ks), n_pool)
    page_table = perm[:n_used].reshape(DEC_BATCH, n_pages).astype(jnp.int32)
    dt0 = jnp.exp(jax.random.uniform(next(ks), (DEPTH, SSM_HEADS), jnp.float32,
                                     minval=math.log(1e-3), maxval=math.log(1e-1)))
    dt_bias = dt0 + jnp.log(-jnp.expm1(-dt0))
    a_log = jnp.log(jax.random.uniform(next(ks), (DEPTH, SSM_HEADS), jnp.float32, minval=1.0, maxval=16.0))
    return {
        "x_prompt": nrm((BATCH, SEQ, D_MODEL), 1.0),
        "x_sample": nrm((DEC_BATCH, DEC_SEQ, D_MODEL), 1.0),
        "cache_sb_k": nrm((DEPTH, n_pool, PAGE_SIZE, SB_HEADS, SB_HEAD_DIM), 1.0),
        "cache_sb_v": nrm((DEPTH, n_pool, PAGE_SIZE, SB_HEADS, SB_HEAD_DIM), 1.0),
        "page_table": page_table,
        "state_sconv": nrm((DEPTH, DEC_BATCH, SC_WIDTH - 1, SC_DIM), 1.0),
        "state_ssm_conv": nrm((DEPTH, DEC_BATCH, SSM_CONV - 1, SSM_CONV_DIM), 1.0),
        "state_ssm": nrm((DEPTH, DEC_BATCH, SSM_HEADS, SSM_HEAD_DIM, SSM_STATE), 0.5),
        "cache_mem_k": nrm((DEPTH, DEC_BATCH, N_MEM, X_HEADS, X_HEAD_DIM), 1.0),
        "cache_mem_v": nrm((DEPTH, DEC_BATCH, N_MEM, X_HEADS, X_HEAD_DIM), 1.0),
        "mem_prompt": nrm((BATCH, N_MEM, D_MODEL), 1.0),
        "g_mix": gain((DEPTH, D_MODEL)),
        "w_in": nrm((DEPTH, D_MODEL, N_IN), D_MODEL ** -0.5),
        "w_sc_conv": nrm((DEPTH, SC_WIDTH, SC_DIM), SC_WIDTH ** -0.5),
        "w_ssm_conv": nrm((DEPTH, SSM_CONV, SSM_CONV_DIM), SSM_CONV ** -0.5),
        "b_ssm_conv": nrm((DEPTH, SSM_CONV_DIM), 0.02),
        "dt_bias": dt_bias,
        "a_log": a_log,
        "d_skip": gain((DEPTH, SSM_HEADS)),
        "g_ssm_norm": gain((DEPTH, SSM_DIM)),
        "sb_bias": SB_BIAS_INIT + nrm((DEPTH, SB_HEADS), 0.1),
        "w_br_sc": nrm((DEPTH, SC_DIM, D_MODEL), SC_DIM ** -0.5),
        "w_br_ssm": nrm((DEPTH, SSM_DIM, D_MODEL), SSM_DIM ** -0.5),
        "w_br_sb": nrm((DEPTH, SB_DIM, D_MODEL), SB_DIM ** -0.5),
        "w_out": nrm((DEPTH, D_MODEL, D_MODEL), D_MODEL ** -0.5),
        "g_x": gain((DEPTH, D_MODEL)),
        "g_mem": gain((DEPTH, D_MODEL)),
        "w_xq": nrm((DEPTH, D_MODEL, D_MODEL), D_MODEL ** -0.5),
        "w_xk": nrm((DEPTH, D_MODEL, D_MODEL), D_MODEL ** -0.5),
        "w_xv": nrm((DEPTH, D_MODEL, D_MODEL), D_MODEL ** -0.5),
        "w_xo": nrm((DEPTH, D_MODEL, D_MODEL), D_MODEL ** -0.5),
        "g_mlp": gain((DEPTH, D_MODEL)),
        "w_ff1": nrm((DEPTH, D_MODEL, D_FF), D_MODEL ** -0.5),
        "w_ff2": nrm((DEPTH, D_FF, D_MODEL), D_FF ** -0.5),
        "g_final": gain((D_MODEL,)),
    }


def reference(x_prompt, x_sample, cache_sb_k, cache_sb_v, page_table, state_sconv, state_ssm_conv,
              state_ssm, cache_mem_k, cache_mem_v, mem_prompt, g_mix, w_in, w_sc_conv, w_ssm_conv,
              b_ssm_conv, dt_bias, a_log, d_skip, g_ssm_norm, sb_bias, w_br_sc, w_br_ssm, w_br_sb, w_out,
              g_x, g_mem, w_xq, w_xk, w_xv, w_xo, g_mlp, w_ff1, w_ff2, g_final):
    xp, xs = x_prompt, x_sample
    bp = xp.shape[0]
    p_k, p_v, p_sc, p_ssc, p_ssm, p_mk, p_mv = [], [], [], [], [], [], []
    s_k, s_v, s_sc, s_ssc, s_ssm = [], [], [], [], []
    for l in range(DEPTH):
        p = {"g_mix": g_mix[l], "w_in": w_in[l], "w_sc_conv": w_sc_conv[l], "w_ssm_conv": w_ssm_conv[l],
             "b_ssm_conv": b_ssm_conv[l], "dt_bias": dt_bias[l], "a_log": a_log[l], "d_skip": d_skip[l],
             "g_ssm_norm": g_ssm_norm[l], "sb_bias": sb_bias[l], "w_br_sc": w_br_sc[l],
             "w_br_ssm": w_br_ssm[l], "w_br_sb": w_br_sb[l], "w_out": w_out[l], "g_x": g_x[l],
             "w_xq": w_xq[l], "w_xo": w_xo[l], "g_mlp": g_mlp[l], "w_ff1": w_ff1[l], "w_ff2": w_ff2[l]}
        mem_n = rmsnorm(mem_prompt, g_mem[l])
        mk = (mem_n @ w_xk[l]).reshape(bp, N_MEM, X_HEADS, X_HEAD_DIM)
        mv = (mem_n @ w_xv[l]).reshape(bp, N_MEM, X_HEADS, X_HEAD_DIM)
        xp, sc_b, ssc_b, st, k4, v4 = trunk_layer(
            xp, p,
            jnp.zeros((bp, SC_WIDTH - 1, SC_DIM), xp.dtype),
            jnp.zeros((bp, SSM_CONV - 1, SSM_CONV_DIM), xp.dtype),
            jnp.zeros((bp, SSM_HEADS, SSM_HEAD_DIM, SSM_STATE), xp.dtype),
            SSM_CHUNK, sb_prompt_attend, mk, mv)
        p_k.append(k4); p_v.append(v4); p_sc.append(sc_b); p_ssc.append(ssc_b); p_ssm.append(st)
        p_mk.append(mk); p_mv.append(mv)
        sample_attend = functools.partial(sb_sample_attend, cache_k=cache_sb_k, cache_v=cache_sb_v,
                                          layer=l, page_table=page_table)
        xs, sc_b, ssc_b, st, k4, v4 = trunk_layer(
            xs, p, state_sconv[l], state_ssm_conv[l], state_ssm[l], xs.shape[1],
            sample_attend, cache_mem_k[l], cache_mem_v[l])
        s_k.append(k4); s_v.append(v4); s_sc.append(sc_b); s_ssc.append(ssc_b); s_ssm.append(st)
    y_prompt = rmsnorm(xp, g_final)
    y_sample = rmsnorm(xs, g_final)
    return (y_prompt, y_sample,
            jnp.stack(p_k), jnp.stack(p_v), jnp.stack(p_sc), jnp.stack(p_ssc), jnp.stack(p_ssm),
            jnp.stack(p_mk), jnp.stack(p_mv),
            jnp.stack(s_k), jnp.stack(s_v), jnp.stack(s_sc), jnp.stack(s_ssc), jnp.stack(s_ssm))
```

```python
import functools

import jax
import jax.numpy as jnp
from jax import lax
from jax.experimental import pallas as pl
from jax.experimental.pallas import tpu as pltpu

F32 = jnp.float32
BF16 = jnp.bfloat16

D_MODEL = 1024
SC_WIDTH = 3
SSM_DIM = 2048
SSM_HEAD_DIM = 64
SSM_HEADS = 32
SSM_GROUPS = 4
SSM_STATE = 128
SSM_GN = SSM_GROUPS * SSM_STATE
SSM_GROUP_DIM = SSM_DIM // SSM_GROUPS
SSM_CONV = 4
SSM_CONV_DIM = SSM_DIM + 2 * SSM_GN
SSM_CHUNK = 128
SB_HEADS = 16
SB_HEAD_DIM = 64
SB_PAIRS = SB_HEADS // 2
SB_SCALE = SB_HEAD_DIM ** -0.5
SB_TQ = 256
SB_TK = 128
X_HEADS = 4
X_HEAD_DIM = 256
X_SCALE = X_HEAD_DIM ** -0.5
N_MEM = 256
D_FF = 4096
EPS = 1e-6
LANES = 128
OFF_DT = 3 * D_MODEL + SSM_DIM + SSM_CONV_DIM
OFF_Q = OFF_DT + SSM_HEADS
N_QKVG = 3 * D_MODEL + 3 * D_MODEL
VMEM_LIMIT = 56 * 1024 * 1024


def _params(*sem):
    return pltpu.CompilerParams(dimension_semantics=sem, vmem_limit_bytes=VMEM_LIMIT)


def _resident(shape):
    nd = len(shape)
    return pl.BlockSpec(shape, lambda *_: (0,) * nd, pipeline_mode=pl.Buffered(1))


def _rms(x, g):
    return x * lax.rsqrt(jnp.mean(x * x, axis=-1, keepdims=True) + EPS) * g


def _softplus(x):
    return jnp.maximum(x, 0.0) + jnp.log1p(jnp.exp(-jnp.abs(x)))


def _silu(x):
    return x * jax.nn.sigmoid(x)


def _split3(x):
    h1 = x.astype(BF16)
    r1 = x - h1.astype(F32)
    h2 = r1.astype(BF16)
    h3 = (r1 - h2.astype(F32)).astype(BF16)
    return h1, h2, h3


def _dot(a, b):
    return jnp.dot(a, b, preferred_element_type=F32)


def _dot_nt(a, b):
    return lax.dot_general(a, b, (((1,), (1,)), ((), ())), preferred_element_type=F32)


def _dot3(x, m):
    h1, h2, h3 = _split3(x)
    return _dot(h1, m) + _dot(h2, m) + _dot(h3, m)


def _norm_mm_kernel(x_ref, g_ref, w_ref, o_ref, hn_ref):
    @pl.when(pl.program_id(1) == 0)
    def _():
        hn_ref[...] = _rms(x_ref[...], g_ref[...]).astype(BF16)

    o_ref[...] = _dot(hn_ref[...], w_ref[...]).astype(o_ref.dtype)


def _norm_matmul(x, g, w, tm, tn, out_dtype=F32):
    m, k = x.shape
    n = w.shape[1]
    return pl.pallas_call(
        _norm_mm_kernel,
        grid=(m // tm, n // tn),
        in_specs=[pl.BlockSpec((tm, k), lambda i, j: (i, 0)),
                  pl.BlockSpec((1, k), lambda i, j: (0, 0)),
                  pl.BlockSpec((k, tn), lambda i, j: (0, j))],
        out_specs=pl.BlockSpec((tm, tn), lambda i, j: (i, j)),
        out_shape=jax.ShapeDtypeStruct((m, n), out_dtype),
        scratch_shapes=[pltpu.VMEM((tm, k), BF16)],
        compiler_params=_params("parallel", "arbitrary"),
        name="norm_matmul",
    )(x, g.reshape(1, k), w)


def _norm_mm_hi_kernel(x_ref, g_ref, w_ref, o_ref):
    hn = _rms(x_ref[...], g_ref[...])
    w = w_ref[...]
    h_hi = hn.astype(BF16)
    h_lo = (hn - h_hi.astype(F32)).astype(BF16)
    w_hi = w.astype(BF16)
    w_lo = (w - w_hi.astype(F32)).astype(BF16)
    o_ref[...] = _dot(h_hi, w_hi) + _dot(h_lo, w_hi) + _dot(h_hi, w_lo)


def _norm_matmul_hi(x, g, w, tm):
    m, k = x.shape
    n = w.shape[1]
    return pl.pallas_call(
        _norm_mm_hi_kernel,
        grid=(m // tm,),
        in_specs=[pl.BlockSpec((tm, k), lambda i: (i, 0)),
                  pl.BlockSpec((1, k), lambda i: (0, 0)),
                  pl.BlockSpec((k, n), lambda i: (0, 0))],
        out_specs=pl.BlockSpec((tm, n), lambda i: (i, 0)),
        out_shape=jax.ShapeDtypeStruct((m, n), F32),
        compiler_params=_params("parallel"),
        name="norm_matmul_hi",
    )(x, g.reshape(1, k), w)


def _sconv_prompt_kernel(b_ref, c_ref, h_ref, w_ref, y_ref, ns_ref, ubuf, *, tm):
    t = pl.program_id(1)
    pad = 8
    keep = SC_WIDTH - 1

    @pl.when(t == 0)
    def _():
        ubuf[0:pad, :] = jnp.zeros((pad, D_MODEL), F32)

    @pl.when(t > 0)
    def _():
        ubuf[0:pad, :] = ubuf[tm:tm + pad, :]

    u = c_ref[...] * h_ref[...]
    ubuf[pad:pad + tm, :] = u
    w = w_ref[...]
    conv = (ubuf[pad - 2:pad - 2 + tm, :] * w[0:1, :] + ubuf[pad - 1:pad - 1 + tm, :] * w[1:2, :]
            + u * w[2:3, :])
    y_ref[...] = (b_ref[...] * conv).astype(BF16)
    ns_ref[...] = ubuf[pad + tm - keep:pad + tm, :]


def _sconv_prompt(p1, w_conv, nb, t, tm):
    nt = t // tm
    blk = lambda c: pl.BlockSpec((tm, D_MODEL), lambda b, i, c=c: (b * nt + i, c))
    return pl.pallas_call(
        functools.partial(_sconv_prompt_kernel, tm=tm),
        grid=(nb, nt),
        in_specs=[blk(0), blk(1), blk(2), pl.BlockSpec((SC_WIDTH, D_MODEL), lambda b, i: (0, 0))],
        out_specs=[pl.BlockSpec((tm, D_MODEL), lambda b, i: (b * nt + i, 0)),
                   pl.BlockSpec((None, SC_WIDTH - 1, D_MODEL), lambda b, i: (b, 0, 0))],
        out_shape=[jax.ShapeDtypeStruct((nb * t, D_MODEL), BF16),
                   jax.ShapeDtypeStruct((nb, SC_WIDTH - 1, D_MODEL), F32)],
        scratch_shapes=[pltpu.VMEM((tm + 8, D_MODEL), F32)],
        compiler_params=_params("parallel", "arbitrary"),
        name="sconv_prompt",
    )(p1, p1, p1, w_conv)


def _ssd_epilogue(y, xs, z, dskip, gnorm):
    y = (y + dskip * xs) * _silu(z)
    outs = []
    for g in range(SSM_GROUPS):
        sl = slice(g * SSM_GROUP_DIM, (g + 1) * SSM_GROUP_DIM)
        outs.append(_rms(y[:, sl], gnorm[:, sl]))
    return jnp.concatenate(outs, axis=-1)


def _ssd_prompt_kernel(xa_ref, xb_ref, xc_ref, za_ref, zb_ref, dt_ref, cw_ref, cb_ref, dtb_ref,
                       alog_ref, dskip_ref, gn_ref, e_ref,
                       y_ref, ncs_ref, nst_ref, xbuf, st_ref, ybuf):
    c = pl.program_id(1)
    nc = pl.num_programs(1)
    L = SSM_CHUNK
    pad = 8
    keep = SSM_CONV - 1

    @pl.when(c == 0)
    def _():
        xbuf[0:pad, :] = jnp.zeros((pad, SSM_CONV_DIM), F32)
        st_ref[...] = jnp.zeros_like(st_ref)

    @pl.when(c > 0)
    def _():
        xbuf[0:pad, :] = xbuf[L:L + pad, :]

    xbuf[pad:pad + L, 0:D_MODEL] = xa_ref[...]
    xbuf[pad:pad + L, D_MODEL:2 * D_MODEL] = xb_ref[...]
    xbuf[pad:pad + L, 2 * D_MODEL:3 * D_MODEL] = xc_ref[...]
    cw = cw_ref[...]
    conv = cb_ref[...] + xbuf[pad:pad + L, :] * cw[keep:keep + 1, :]
    for j in range(keep):
        conv = conv + xbuf[pad - keep + j:pad - keep + j + L, :] * cw[j:j + 1, :]
    xbc = _silu(conv)
    ncs_ref[...] = xbuf[pad + L - keep:pad + L, :]

    xs = xbc[:, :SSM_DIM]
    bm = xbc[:, SSM_DIM:SSM_DIM + SSM_GN]
    cm = xbc[:, SSM_DIM + SSM_GN:]

    row = lax.broadcasted_iota(jnp.int32, (L, L), 0)
    col = lax.broadcasted_iota(jnp.int32, (L, L), 1)
    causal = col <= row
    tri = jnp.where(causal, 1.0, 0.0).astype(BF16)

    dt = _softplus(dt_ref[...] + dtb_ref[...])
    a = -jnp.exp(alog_ref[...])
    adt = a * dt
    h1, h2, h3 = _split3(adt)
    a_cs = _dot(tri, h1) + _dot(tri, h2) + _dot(tri, h3)
    a_cs_t = a_cs.T
    ea = jnp.exp(a_cs)
    dte = jnp.exp(a_cs[L - 1:L, :] - a_cs)
    e = e_ref[...]
    dt_x = _dot3(dt, e)
    ea_x = _dot3(ea, e)
    dte_x = _dot3(dte, e)
    xdt = xs * dt_x
    xdt_b = xdt.astype(BF16)
    xdec_b = (xdt * dte_x).astype(BF16)
    lane = lax.broadcasted_iota(jnp.int32, (L, LANES), 1)
    first_half = lane < SSM_HEAD_DIM

    for g in range(SSM_GROUPS):
        gs = slice(g * SSM_STATE, (g + 1) * SSM_STATE)
        gd = slice(g * SSM_GROUP_DIM, (g + 1) * SSM_GROUP_DIM)
        b_g = bm[:, gs]
        c_b = cm[:, gs].astype(BF16)
        cb = _dot_nt(c_b, b_g.astype(BF16))
        st_g = st_ref[g]
        ybuf[:, gd] = _dot(c_b, st_g.astype(BF16)) * ea_x[:, gd]
        for pr in range(SSM_GROUP_DIM // LANES):
            res = []
            for half in range(2):
                h = g * (SSM_HEADS // SSM_GROUPS) + 2 * pr + half
                seg = jnp.exp(jnp.where(causal, a_cs[:, h:h + 1] - a_cs_t[h:h + 1, :], -jnp.inf))
                res.append((cb * seg).astype(BF16))
            ps = slice(g * SSM_GROUP_DIM + pr * LANES, g * SSM_GROUP_DIM + (pr + 1) * LANES)
            x2 = xdt_b[:, ps]
            ybuf[:, ps] += jnp.where(first_half, _dot(res[0], x2), _dot(res[1], x2))
        st_ref[g] = st_g * ea_x[L - 1:L, gd] + _dot(b_g.T.astype(BF16), xdec_b[:, gd])

    z = jnp.concatenate([za_ref[...], zb_ref[...]], axis=-1)
    y_ref[...] = _ssd_epilogue(ybuf[...], xs, z, dskip_ref[...], gn_ref[...]).astype(BF16)

    @pl.when(c == nc - 1)
    def _():
        for g in range(SSM_GROUPS):
            nst_ref[g * SSM_GROUP_DIM:(g + 1) * SSM_GROUP_DIM, :] = st_ref[g].T


def _ssd_prompt(p1, pdt, cw, cb, dtb, alog, dskip_x, gn, e, nb, t):
    L = SSM_CHUNK
    nc = t // L
    blk = lambda c: pl.BlockSpec((L, D_MODEL), lambda b, i, c=c: (b * nc + i, c))
    full = lambda r, n: pl.BlockSpec((r, n), lambda b, i: (0, 0))
    return pl.pallas_call(
        _ssd_prompt_kernel,
        grid=(nb, nc),
        in_specs=[blk(5), blk(6), blk(7), blk(3), blk(4),
                  pl.BlockSpec((L, LANES), lambda b, i: (b * nc + i, 0)),
                  full(SSM_CONV, SSM_CONV_DIM), full(1, SSM_CONV_DIM), full(1, LANES), full(1, LANES),
                  full(1, SSM_DIM), full(1, SSM_DIM), full(LANES, SSM_DIM)],
        out_specs=[pl.BlockSpec((L, SSM_DIM), lambda b, i: (b * nc + i, 0)),
                   pl.BlockSpec((None, SSM_CONV - 1, SSM_CONV_DIM), lambda b, i: (b, 0, 0)),
                   pl.BlockSpec((None, SSM_DIM, SSM_STATE), lambda b, i: (b, 0, 0))],
        out_shape=[jax.ShapeDtypeStruct((nb * t, SSM_DIM), BF16),
                   jax.ShapeDtypeStruct((nb, SSM_CONV - 1, SSM_CONV_DIM), F32),
                   jax.ShapeDtypeStruct((nb, SSM_DIM, SSM_STATE), F32)],
        scratch_shapes=[pltpu.VMEM((L + 8, SSM_CONV_DIM), F32),
                        pltpu.VMEM((SSM_GROUPS, SSM_STATE, SSM_GROUP_DIM), F32),
                        pltpu.VMEM((L, SSM_DIM), F32)],
        compiler_params=_params("parallel", "arbitrary"),
        name="ssd_prompt",
    )(p1, p1, p1, p1, p1, pdt, cw, cb, dtb, alog, dskip_x, gn, e)


def _sb_prompt_kernel(bias_ref, q_ref, k_ref, v_ref, o_ref, acc_ref, tot_ref):
    p = pl.program_id(1)
    i = pl.program_id(2)
    tq, tk = SB_TQ, SB_TK
    lane = lax.broadcasted_iota(jnp.int32, (1, LANES), 1)
    head0 = lane < SB_HEAD_DIM
    lane2 = lax.broadcasted_iota(jnp.int32, (1, 2 * tk), 1)
    bias = jnp.where(lane2 < tk, bias_ref[2 * p], bias_ref[2 * p + 1])
    r2 = lax.broadcasted_iota(jnp.int32, (2 * tk, 2 * tk), 0)
    c2 = lax.broadcasted_iota(jnp.int32, (2 * tk, 2 * tk), 1)
    same = (r2 < tk) == (c2 < tk)
    m_tot = jnp.where(same, 1.0, 0.0).astype(BF16)
    m_cum = jnp.where(same & (r2 >= c2), 1.0, 0.0).astype(BF16)
    qb = (q_ref[...] * SB_SCALE).astype(BF16)
    acc_ref[...] = jnp.zeros_like(acc_ref)
    tot_ref[...] = jnp.zeros_like(tot_ref)
    q_pos = i * tq + lax.broadcasted_iota(jnp.int32, (tq, 2 * tk), 0)
    k_off = lax.broadcasted_iota(jnp.int32, (tq, 2 * tk), 1) % tk

    def tile(jb, masked):
        start = pl.multiple_of(jb * tk, tk)
        kblk = k_ref[pl.ds(start, tk), :]
        vblk = v_ref[pl.ds(start, tk), :]
        kstack = jnp.concatenate([jnp.where(head0, kblk, 0.0), jnp.where(head0, 0.0, kblk)], axis=0).astype(BF16)
        vstack = jnp.concatenate([jnp.where(head0, vblk, 0.0), jnp.where(head0, 0.0, vblk)], axis=0).astype(BF16)
        z = _dot_nt(qb, kstack) + bias
        sp = _softplus(z)
        if masked:
            valid = (jb * tk + k_off) < q_pos
            sp = jnp.where(valid, sp, 0.0)
        sp_b = sp.astype(BF16)
        later = tot_ref[...] + _dot(sp_b, m_cum)
        w = jnp.exp(z - later)
        if masked:
            w = jnp.where(valid, w, 0.0)
        acc_ref[...] += _dot(w.astype(BF16), vstack)
        tot_ref[...] += _dot(sp_b, m_tot)

    n_diag = tq // tk
    for d in range(n_diag):
        tile(i * n_diag + (n_diag - 1 - d), True)

    def body(j, carry):
        tile(i * n_diag - 1 - j, False)
        return carry

    lax.fori_loop(0, i * n_diag, body, 0)
    o_ref[...] = acc_ref[...].astype(BF16)


def _sb_prompt(p2, sb_bias, nb, t):
    nq = t // SB_TQ
    return pl.pallas_call(
        _sb_prompt_kernel,
        grid=(nb, SB_PAIRS, nq),
        in_specs=[pl.BlockSpec(memory_space=pltpu.SMEM),
                  pl.BlockSpec((SB_TQ, LANES), lambda b, p, i: (b * nq + i, p)),
                  pl.BlockSpec((t, LANES), lambda b, p, i: (b, SB_PAIRS + p)),
                  pl.BlockSpec((t, LANES), lambda b, p, i: (b, 2 * SB_PAIRS + p))],
        out_specs=pl.BlockSpec((SB_TQ, LANES), lambda b, p, i: (b * nq + i, p)),
        out_shape=jax.ShapeDtypeStruct((nb * t, D_MODEL), BF16),
        scratch_shapes=[pltpu.VMEM((SB_TQ, LANES), F32), pltpu.VMEM((SB_TQ, 2 * SB_TK), F32)],
        compiler_params=_params("parallel", "parallel", "arbitrary"),
        name="sb_prompt",
    )(sb_bias, p2, p2, p2)


def _sb_decode_kernel(pt_ref, bias_ref, q_ref, kn_ref, vn_ref, *rest, pps, n_pages):
    k_refs = rest[:pps]
    v_refs = rest[pps:2 * pps]
    o_ref, qrow_ref, tot_ref, acc_ref = rest[2 * pps:]
    s = pl.program_id(1)
    page = SB_TK
    hrow = lax.broadcasted_iota(jnp.int32, (SB_HEADS, D_MODEL), 0)
    hcol = lax.broadcasted_iota(jnp.int32, (SB_HEADS, D_MODEL), 1) // SB_HEAD_DIM
    own = hrow == hcol
    bias = bias_ref[...]
    r1 = lax.broadcasted_iota(jnp.int32, (page, page), 0)
    c1 = lax.broadcasted_iota(jnp.int32, (page, page), 1)
    m_cum = jnp.where(r1 >= c1, 1.0, 0.0).astype(BF16)
    m_tot = jnp.ones((page, page), BF16)

    @pl.when(s == 0)
    def _():
        q_own = jnp.where(own, q_ref[...] * SB_SCALE, 0.0)
        qrow_ref[...] = q_own.astype(BF16)
        past = n_pages * page
        valid = jnp.full((SB_HEADS, LANES), past, jnp.int32) < jnp.full((SB_HEADS, LANES), past, jnp.int32)
        z_new = jnp.sum(q_own.astype(BF16).astype(F32) * kn_ref[...].astype(BF16).astype(F32),
                        axis=-1, keepdims=True) + bias
        sp_new = jnp.where(valid, _softplus(z_new), 0.0)
        w_new = jnp.where(valid, jnp.exp(z_new - sp_new), 0.0)
        tot_ref[...] = sp_new
        acc_ref[...] = w_new[:, 0:1] * jnp.where(own, vn_ref[...], 0.0)

    qrow = qrow_ref[...]
    for r in range(pps):
        kb = k_refs[r][...].astype(BF16)
        vb = v_refs[r][...].astype(BF16)
        z = _dot_nt(qrow, kb) + bias
        sp = _softplus(z)
        sp_hi = sp.astype(BF16)
        sp_lo = (sp - sp_hi.astype(F32)).astype(BF16)
        later = tot_ref[...] + _dot(sp_hi, m_cum) + _dot(sp_lo, m_cum)
        w = jnp.exp(z - later)
        acc_ref[...] += _dot(w.astype(BF16), vb)
        tot_ref[...] += _dot(sp_hi, m_tot) + _dot(sp_lo, m_tot)

    @pl.when(s == pl.num_programs(1) - 1)
    def _():
        o_ref[...] = jnp.sum(jnp.where(own, acc_ref[...], 0.0), axis=0, keepdims=True).astype(BF16)


def _sb_decode(p2s, sb_bias_rep, cache_k, cache_v, page_table, layer, pps):
    ns = p2s.shape[0]
    n_pages = page_table.shape[1]
    n_steps = n_pages // pps
    p2s3 = p2s.reshape(ns, 1, N_QKVG)
    row = lambda c: pl.BlockSpec((None, 1, D_MODEL), lambda n, s, pt, c=c: (n, 0, c))

    def page_spec(r):
        def imap(n, s, pt):
            return (layer, pt[n * n_pages + (n_pages - 1 - (s * pps + r))], 0, 0)
        return pl.BlockSpec((None, None, SB_TK, D_MODEL), imap)

    specs = [pl.BlockSpec((SB_HEADS, LANES), lambda n, s, pt: (0, 0)), row(0), row(1), row(2)]
    specs += [page_spec(r) for r in range(pps)] * 2
    grid_spec = pltpu.PrefetchScalarGridSpec(
        num_scalar_prefetch=1,
        grid=(ns, n_steps),
        in_specs=specs,
        out_specs=pl.BlockSpec((None, 1, D_MODEL), lambda n, s, pt: (n, 0, 0)),
        scratch_shapes=[pltpu.VMEM((SB_HEADS, D_MODEL), BF16), pltpu.VMEM((SB_HEADS, LANES), F32),
                        pltpu.VMEM((SB_HEADS, D_MODEL), F32)])
    out = pl.pallas_call(
        functools.partial(_sb_decode_kernel, pps=pps, n_pages=n_pages),
        grid_spec=grid_spec,
        out_shape=jax.ShapeDtypeStruct((ns, 1, D_MODEL), BF16),
        compiler_params=_params("parallel", "arbitrary"),
        name="sb_decode",
    )(page_table.reshape(-1), sb_bias_rep, p2s3, p2s3, p2s3, *([cache_k] * pps), *([cache_v] * pps))
    return out.reshape(ns, D_MODEL)


def _sample_pre_kernel(p1_ref, dt_ref, ssc_ref, sxc_ref, scw_ref, cw_ref, cb_ref, dtb_ref, alog_ref, e_ref,
                       ya_ref, nsc_ref, nxc_ref, xs_ref, b_ref, c_ref, dtx_ref, dec_ref):
    d = D_MODEL
    p1 = p1_ref[...]
    u = p1[:, d:2 * d] * p1[:, 2 * d:3 * d]
    ssc = ssc_ref[...]
    scw = scw_ref[...]
    conv_a = ssc[:, 0:d] * scw[0:1, :] + ssc[:, d:2 * d] * scw[1:2, :] + u * scw[2:3, :]
    ya_ref[...] = (p1[:, 0:d] * conv_a).astype(BF16)
    nsc_ref[:, 0:d] = ssc[:, d:2 * d]
    nsc_ref[:, d:2 * d] = u
    cd = SSM_CONV_DIM
    xbc = p1[:, 3 * d + SSM_DIM:3 * d + SSM_DIM + cd]
    sxc = sxc_ref[...]
    cw = cw_ref[...]
    conv = cb_ref[...] + xbc * cw[SSM_CONV - 1:SSM_CONV, :]
    for j in range(SSM_CONV - 1):
        conv = conv + sxc[:, j * cd:(j + 1) * cd] * cw[j:j + 1, :]
        if j > 0:
            nxc_ref[:, (j - 1) * cd:j * cd] = sxc[:, j * cd:(j + 1) * cd]
    nxc_ref[:, (SSM_CONV - 2) * cd:(SSM_CONV - 1) * cd] = xbc
    act = _silu(conv)
    xs = act[:, :SSM_DIM]
    xs_ref[...] = xs
    b_ref[...] = act[:, SSM_DIM:SSM_DIM + SSM_GN]
    c_ref[...] = act[:, SSM_DIM + SSM_GN:]
    dt = _softplus(dt_ref[...] + dtb_ref[...])
    dec = jnp.exp(-jnp.exp(alog_ref[...]) * dt)
    e = e_ref[...]
    dtx_ref[...] = xs * _dot3(dt, e)
    dec_ref[...] = _dot3(dec, e)


def _sample_pre(p1s, pdts, ssc, sxc, scw, cw, cb, dtb, alog, e):
    ns = p1s.shape[0]
    f = lambda n: jax.ShapeDtypeStruct((ns, n), F32)
    return pl.pallas_call(
        _sample_pre_kernel,
        out_shape=[jax.ShapeDtypeStruct((ns, D_MODEL), BF16), f((SC_WIDTH - 1) * D_MODEL),
                   f((SSM_CONV - 1) * SSM_CONV_DIM), f(SSM_DIM), f(SSM_GN), f(SSM_GN), f(SSM_DIM), f(SSM_DIM)],
        compiler_params=pltpu.CompilerParams(vmem_limit_bytes=VMEM_LIMIT),
        name="sample_pre",
    )(p1s, pdts, ssc, sxc, scw, cw, cb, dtb, alog, e)


def _sample_state_kernel(st_ref, dtx_ref, dec_ref, b_ref, c_ref, nst_ref, y_ref):
    nblk = SSM_DIM // LANES
    zpad = jnp.zeros((LANES - nblk, LANES), F32)
    dtx_t = jnp.concatenate([dtx_ref[...], zpad], axis=0).T
    dec_t = jnp.concatenate([dec_ref[...], zpad], axis=0).T
    b = b_ref[...]
    c = c_ref[...]
    lane = lax.broadcasted_iota(jnp.int32, (LANES, LANES), 1)
    ycols = jnp.zeros((LANES, LANES), F32)
    for i in range(nblk):
        g = i // (nblk // SSM_GROUPS)
        rows = slice(i * LANES, (i + 1) * LANES)
        new = st_ref[rows, :] * dec_t[:, i:i + 1] + dtx_t[:, i:i + 1] * b[g:g + 1, :]
        nst_ref[rows, :] = new
        ycol = jnp.sum(new * c[g:g + 1, :], axis=-1, keepdims=True)
        ycols = jnp.where(lane == i, ycol, ycols)
    y_ref[...] = ycols.T[0:nblk, :]


def _sample_state(state, dtx, dec, b, c):
    ns = state.shape[0]
    nblk = SSM_DIM // LANES
    vec = lambda r: pl.BlockSpec((None, r, LANES), lambda n: (n, 0, 0))
    st_spec = pl.BlockSpec((None, SSM_DIM, SSM_STATE), lambda n: (n, 0, 0))
    nst, y = pl.pallas_call(
        _sample_state_kernel,
        grid=(ns,),
        in_specs=[st_spec, vec(nblk), vec(nblk), vec(SSM_GROUPS), vec(SSM_GROUPS)],
        out_specs=[st_spec, vec(nblk)],
        out_shape=[jax.ShapeDtypeStruct((ns, SSM_DIM, SSM_STATE), F32),
                   jax.ShapeDtypeStruct((ns, nblk, LANES), F32)],
        compiler_params=_params("parallel"),
        name="sample_state",
    )(state.reshape(ns, SSM_DIM, SSM_STATE), dtx.reshape(ns, nblk, LANES), dec.reshape(ns, nblk, LANES),
      b.reshape(ns, SSM_GROUPS, SSM_STATE), c.reshape(ns, SSM_GROUPS, SSM_STATE))
    return nst, y.reshape(ns, SSM_DIM)


def _sample_post_kernel(y_ref, xs_ref, p1_ref, dskip_ref, gn_ref, o_ref):
    z = p1_ref[:, 3 * D_MODEL:3 * D_MODEL + SSM_DIM]
    o_ref[...] = _ssd_epilogue(y_ref[...], xs_ref[...], z, dskip_ref[...], gn_ref[...]).astype(BF16)


def _sample_post(y, xs, p1s, dskip_x, gn):
    return pl.pallas_call(
        _sample_post_kernel,
        out_shape=jax.ShapeDtypeStruct(y.shape, BF16),
        compiler_params=pltpu.CompilerParams(vmem_limit_bytes=VMEM_LIMIT),
        name="sample_post",
    )(y, xs, p1s, dskip_x, gn)


def _merge_kernel(x_ref, ya_ref, yb_ref, yc_ref, g0_ref, g1_ref, g2_ref, wa_ref, wb_ref, wc_ref, wo_ref,
                  gx_ref, wq_ref, x1_ref, q_ref):
    merged = (jax.nn.sigmoid(g0_ref[...]) * _dot(ya_ref[...], wa_ref[...])
              + jax.nn.sigmoid(g1_ref[...]) * _dot(yb_ref[...], wb_ref[...])
              + jax.nn.sigmoid(g2_ref[...]) * _dot(yc_ref[...], wc_ref[...]))
    x1 = x_ref[...] + _dot(merged.astype(BF16), wo_ref[...])
    x1_ref[...] = x1
    hn = _rms(x1, gx_ref[...]).astype(BF16)
    q_ref[...] = (_dot(hn, wq_ref[...]) * X_SCALE).astype(BF16)


def _merge(x, ya, yb, yc, p2, wa, wb, wc, wo, gx, wq, tm):
    m = x.shape[0]
    d = D_MODEL
    rows = lambda n, c=0: pl.BlockSpec((tm, n), lambda i, c=c: (i, c))
    return pl.pallas_call(
        _merge_kernel,
        grid=(m // tm,),
        in_specs=[rows(d), rows(d), rows(SSM_DIM), rows(d), rows(d, 3), rows(d, 4), rows(d, 5),
                  _resident((d, d)), _resident((SSM_DIM, d)), _resident((d, d)), _resident((d, d)),
                  _resident((1, d)), _resident((d, d))],
        out_specs=[rows(d), rows(d)],
        out_shape=[jax.ShapeDtypeStruct((m, d), F32), jax.ShapeDtypeStruct((m, d), BF16)],
        compiler_params=_params("parallel"),
        name="merge",
    )(x, ya, yb, yc, p2, p2, p2, wa, wb, wc, wo, gx.reshape(1, d), wq)


def _mem_attn_kernel(q_ref, k_ref, v_ref, o_ref):
    q = q_ref[...]
    outs = []
    for h in range(X_HEADS):
        hs = slice(h * X_HEAD_DIM, (h + 1) * X_HEAD_DIM)
        s = _dot_nt(q[:, hs], k_ref[:, hs].astype(BF16))
        e = jnp.exp(s - jnp.max(s, axis=-1, keepdims=True))
        prob = e / jnp.sum(e, axis=-1, keepdims=True)
        outs.append(_dot(prob.astype(BF16), v_ref[:, hs].astype(BF16)))
    o_ref[...] = jnp.concatenate(outs, axis=-1).astype(BF16)


def _mem_attn(q, mem_k, mem_v, tm):
    nb, t, d = q.shape
    kv = pl.BlockSpec((None, N_MEM, d), lambda b, i: (b, 0, 0))
    qs = pl.BlockSpec((None, tm, d), lambda b, i: (b, i, 0))
    return pl.pallas_call(
        _mem_attn_kernel,
        grid=(nb, t // tm),
        in_specs=[qs, kv, kv],
        out_specs=qs,
        out_shape=jax.ShapeDtypeStruct((nb, t, d), BF16),
        compiler_params=_params("parallel", "arbitrary"),
        name="mem_attn",
    )(q, mem_k, mem_v)


def _mlp_kernel(x1_ref, at_ref, wxo_ref, gm_ref, w1_ref, w2_ref, gf_ref, o_ref, *, final):
    x2 = x1_ref[...] + _dot(at_ref[...], wxo_ref[...])
    hn = _rms(x2, gm_ref[...]).astype(BF16)
    acc = x2
    fc = 1024
    for c in range(D_FF // fc):
        h = jnp.maximum(_dot(hn, w1_ref[:, c * fc:(c + 1) * fc]), 0.0)
        acc = acc + _dot((h * h).astype(BF16), w2_ref[c * fc:(c + 1) * fc, :])
    o_ref[...] = _rms(acc, gf_ref[...]) if final else acc


def _mlp(x1, attn, wxo, gm, w1, w2, gf, tm, final):
    m, d = x1.shape
    rows = pl.BlockSpec((tm, d), lambda i: (i, 0))
    return pl.pallas_call(
        functools.partial(_mlp_kernel, final=final),
        grid=(m // tm,),
        in_specs=[rows, rows, _resident((d, d)), _resident((1, d)), _resident((d, D_FF)),
                  _resident((D_FF, d)), _resident((1, d))],
        out_specs=rows,
        out_shape=jax.ShapeDtypeStruct((m, d), F32),
        compiler_params=_params("parallel"),
        name="mlp",
    )(x1, attn, wxo, gm.reshape(1, d), w1, w2, gf.reshape(1, d))


def _pad_lanes(v):
    return jnp.pad(v.astype(F32), (0, LANES - v.shape[0])).reshape(1, LANES)


def kernel(x_prompt, x_sample, cache_sb_k, cache_sb_v, page_table, state_sconv, state_ssm_conv, state_ssm, cache_mem_k, cache_mem_v, mem_prompt, g_mix, w_in, w_sc_conv, w_ssm_conv, b_ssm_conv, dt_bias, a_log, d_skip, g_ssm_norm, sb_bias, w_br_sc, w_br_ssm, w_br_sb, w_out, g_x, g_mem, w_xq, w_xk, w_xv, w_xo, g_mlp, w_ff1, w_ff2, g_final):
    nb, t, d = x_prompt.shape
    ns = x_sample.shape[0]
    depth = w_in.shape[0]
    n_pool = cache_sb_k.shape[1]
    n_pages = page_table.shape[1]
    mp = nb * t
    tm_p = min(1024, t)
    pps = 8 if n_pages % 8 == 0 else n_pages

    xp = x_prompt.reshape(mp, d)
    xs = x_sample.reshape(ns, d)
    mem2 = mem_prompt.reshape(nb * N_MEM, d)
    ck = cache_sb_k.reshape(depth, n_pool, SB_TK, d)
    cv = cache_sb_v.reshape(depth, n_pool, SB_TK, d)
    head_of_channel = jnp.arange(SSM_DIM) // SSM_HEAD_DIM
    expand = (jnp.arange(LANES)[:, None] == head_of_channel[None, :]).astype(BF16)

    outs = {k: [] for k in ("p_k", "p_v", "p_sc", "p_ssc", "p_ssm", "p_mk", "p_mv",
                            "s_k", "s_v", "s_sc", "s_ssc", "s_ssm")}
    for l in range(depth):
        final = l == depth - 1
        w1 = w_in[l, :, :OFF_DT].astype(BF16)
        w2 = w_in[l, :, OFF_Q:].astype(BF16)
        wdt = jnp.pad(w_in[l, :, OFF_DT:OFF_Q], ((0, 0), (0, LANES - SSM_HEADS)))
        wa, wb, wc, wo = (w[l].astype(BF16) for w in (w_br_sc, w_br_ssm, w_br_sb, w_out))
        wq, wxo, wf1, wf2 = (w[l].astype(BF16) for w in (w_xq, w_xo, w_ff1, w_ff2))
        cb = b_ssm_conv[l].reshape(1, SSM_CONV_DIM)
        dtb = _pad_lanes(dt_bias[l])
        alog = _pad_lanes(a_log[l])
        dskip_x = d_skip[l].astype(F32)[head_of_channel].reshape(1, SSM_DIM)
        gn = g_ssm_norm[l].reshape(1, SSM_DIM)
        bias_rep = jnp.broadcast_to(sb_bias[l].astype(F32)[:, None], (SB_HEADS, LANES))

        mk = _norm_matmul(mem2, g_mem[l], w_xk[l].astype(BF16), tm=min(1024, nb * N_MEM), tn=512)
        mv = _norm_matmul(mem2, g_mem[l], w_xv[l].astype(BF16), tm=min(1024, nb * N_MEM), tn=512)
        p1 = _norm_matmul(xp, g_mix[l], w1, tm=tm_p, tn=512)
        p2 = _norm_matmul(xp, g_mix[l], w2, tm=tm_p, tn=512)
        pdt = _norm_matmul_hi(xp, g_mix[l], wdt, tm=tm_p)
        ya, p_sc = _sconv_prompt(p1, w_sc_conv[l], nb, t, tm=min(512, t))
        yb, p_ssc, p_ssm = _ssd_prompt(p1, pdt, w_ssm_conv[l], cb, dtb, alog, dskip_x, gn, expand, nb, t)
        yc = _sb_prompt(p2, sb_bias[l].astype(F32), nb, t)
        x1, qx = _merge(xp, ya, yb, yc, p2, wa, wb, wc, wo, g_x[l], wq, tm=256)
        attn = _mem_attn(qx.reshape(nb, t, d), mk.reshape(nb, N_MEM, d), mv.reshape(nb, N_MEM, d), tm=min(512, t))
        xp = _mlp(x1, attn.reshape(mp, d), wxo, g_mlp[l], wf1, wf2, g_final, tm=256, final=final)
        outs["p_k"].append(p2[:, d:2 * d].reshape(nb, t, SB_HEADS, SB_HEAD_DIM))
        outs["p_v"].append(p2[:, 2 * d:3 * d].reshape(nb, t, SB_HEADS, SB_HEAD_DIM))
        outs["p_sc"].append(p_sc)
        outs["p_ssc"].append(p_ssc)
        outs["p_ssm"].append(p_ssm.reshape(nb, SSM_HEADS, SSM_HEAD_DIM, SSM_STATE))
        outs["p_mk"].append(mk.reshape(nb, N_MEM, X_HEADS, X_HEAD_DIM))
        outs["p_mv"].append(mv.reshape(nb, N_MEM, X_HEADS, X_HEAD_DIM))

        p1s = _norm_matmul(xs, g_mix[l], w1, tm=ns, tn=512)
        p2s = _norm_matmul(xs, g_mix[l], w2, tm=ns, tn=512)
        pdts = _norm_matmul_hi(xs, g_mix[l], wdt, tm=ns)
        ya_s, s_sc, s_ssc, xs_c, b_c, c_c, dtx, dec = _sample_pre(
            p1s, pdts, state_sconv[l].reshape(ns, -1), state_ssm_conv[l].reshape(ns, -1),
            w_sc_conv[l], w_ssm_conv[l], cb, dtb, alog, expand)
        s_ssm, y_s = _sample_state(state_ssm[l], dtx, dec, b_c, c_c)
        yb_s = _sample_post(y_s, xs_c, p1s, dskip_x, gn)
        yc_s = _sb_decode(p2s, bias_rep, ck, cv, page_table, l, pps)
        x1s, qxs = _merge(xs, ya_s, yb_s, yc_s, p2s, wa, wb, wc, wo, g_x[l], wq, tm=ns)
        attn_s = _mem_attn(qxs.reshape(ns, 1, d), cache_mem_k[l].reshape(ns, N_MEM, d),
                           cache_mem_v[l].reshape(ns, N_MEM, d), tm=1)
        xs = _mlp(x1s, attn_s.reshape(ns, d), wxo, g_mlp[l], wf1, wf2, g_final, tm=ns, final=final)
        outs["s_k"].append(p2s[:, d:2 * d].reshape(ns, 1, SB_HEADS, SB_HEAD_DIM))
        outs["s_v"].append(p2s[:, 2 * d:3 * d].reshape(ns, 1, SB_HEADS, SB_HEAD_DIM))
        outs["s_sc"].append(s_sc.reshape(ns, SC_WIDTH - 1, D_MODEL))
        outs["s_ssc"].append(s_ssc.reshape(ns, SSM_CONV - 1, SSM_CONV_DIM))
        outs["s_ssm"].append(s_ssm.reshape(ns, SSM_HEADS, SSM_HEAD_DIM, SSM_STATE))

    st = {k: jnp.stack(v) for k, v in outs.items()}
    return (xp.reshape(nb, t, d), xs.reshape(ns, 1, d),
            st["p_k"], st["p_v"], st["p_sc"], st["p_ssc"], st["p_ssm"], st["p_mk"], st["p_mv"],
            st["s_k"], st["s_v"], st["s_sc"], st["s_ssc"], st["s_ssm"])
```

```python
import functools

import jax
import jax.numpy as jnp
from jax import lax
from jax.experimental import pallas as pl
from jax.experimental.pallas import tpu as pltpu

F32 = jnp.float32
BF16 = jnp.bfloat16

D_MODEL = 1024
SC_WIDTH = 3
SSM_DIM = 2048
SSM_HEAD_DIM = 64
SSM_HEADS = 32
SSM_GROUPS = 4
SSM_STATE = 128
SSM_GN = SSM_GROUPS * SSM_STATE
SSM_GROUP_DIM = SSM_DIM // SSM_GROUPS
SSM_CONV = 4
SSM_CONV_DIM = SSM_DIM + 2 * SSM_GN
SSM_CHUNK = 128
SB_HEADS = 16
SB_HEAD_DIM = 64
SB_SCALE = SB_HEAD_DIM ** -0.5
SB_TQ = 256
SB_TK = 128
SB_GROUP = 4
LOG2E = 1.4426950408889634
X_HEADS = 4
X_HEAD_DIM = 256
X_SCALE = X_HEAD_DIM ** -0.5
N_MEM = 256
D_FF = 4096
EPS = 1e-6
LANES = 128
OFF_DT = 3 * D_MODEL + SSM_DIM + SSM_CONV_DIM
OFF_Q = OFF_DT + SSM_HEADS
OFF_GATES = OFF_Q + 3 * D_MODEL
VMEM_LIMIT = 56 * 1024 * 1024


def _params(*sem):
    return pltpu.CompilerParams(dimension_semantics=sem, vmem_limit_bytes=VMEM_LIMIT)


def _resident(shape):
    nd = len(shape)
    return pl.BlockSpec(shape, lambda *_: (0,) * nd, pipeline_mode=pl.Buffered(1))


def _rms(x, g):
    return x * lax.rsqrt(jnp.mean(x * x, axis=-1, keepdims=True) + EPS) * g


def _softplus(x):
    return jnp.maximum(x, 0.0) + jnp.log1p(jnp.exp(-jnp.abs(x)))


def _silu(x):
    return x * jax.nn.sigmoid(x)


def _split3(x):
    h1 = x.astype(BF16)
    r1 = x - h1.astype(F32)
    h2 = r1.astype(BF16)
    h3 = (r1 - h2.astype(F32)).astype(BF16)
    return h1, h2, h3


def _split2(x):
    hi = x.astype(BF16)
    return hi, (x - hi.astype(F32)).astype(BF16)


def _dot(a, b):
    return jnp.dot(a, b, preferred_element_type=F32)


def _dot_nt(a, b):
    return lax.dot_general(a, b, (((1,), (1,)), ((), ())), preferred_element_type=F32)


def _dot3(x, m):
    h1, h2, h3 = _split3(x)
    return _dot(h1, m) + _dot(h2, m) + _dot(h3, m)


def _dot2_left(m, x):
    hi, lo = _split2(x)
    return _dot(m, hi) + _dot(m, lo)


def _dot2_right(x, m):
    hi, lo = _split2(x)
    return _dot(hi, m) + _dot(lo, m)


def _norm_mm_kernel(x_ref, g_ref, w_ref, o_ref, hn_ref, *, out_scale):
    @pl.when(pl.program_id(1) == 0)
    def _():
        hn_ref[...] = _rms(x_ref[...], g_ref[...]).astype(BF16)

    r = _dot(hn_ref[...], w_ref[...])
    if out_scale is not None:
        r = r * out_scale
    o_ref[...] = r.astype(o_ref.dtype)


def _norm_matmul(x, g, w, tm, tn, out_dtype=F32, out_scale=None):
    m, k = x.shape
    n = w.shape[1]
    return pl.pallas_call(
        functools.partial(_norm_mm_kernel, out_scale=out_scale),
        grid=(m // tm, n // tn),
        in_specs=[pl.BlockSpec((tm, k), lambda i, j: (i, 0)),
                  pl.BlockSpec((1, k), lambda i, j: (0, 0)),
                  pl.BlockSpec((k, tn), lambda i, j: (0, j))],
        out_specs=pl.BlockSpec((tm, tn), lambda i, j: (i, j)),
        out_shape=jax.ShapeDtypeStruct((m, n), out_dtype),
        scratch_shapes=[pltpu.VMEM((tm, k), BF16)],
        compiler_params=_params("parallel", "arbitrary"),
        name="norm_matmul",
    )(x, g.reshape(1, k), w)


def _norm_mm_kv_kernel(x_ref, g_ref, w_ref, o32_ref, o16_ref, hn_ref, *, tm, tn):
    @pl.when(pl.program_id(1) == 0)
    def _():
        hn_ref[...] = _rms(x_ref[...], g_ref[...]).astype(BF16)

    r = _dot(hn_ref[...], w_ref[...])
    o32_ref[...] = r
    even = (lax.broadcasted_iota(jnp.int32, (1, tn), 1) // SB_HEAD_DIM) % 2 == 0
    for a in range(tm // SB_TK):
        blk = r[a * SB_TK:(a + 1) * SB_TK, :]
        o16_ref[a, 0] = jnp.where(even, blk, 0.0).astype(BF16)
        o16_ref[a, 1] = jnp.where(even, 0.0, blk).astype(BF16)


def _norm_matmul_kv(x, g, w, tm, tn):
    m, k = x.shape
    n = w.shape[1]
    return pl.pallas_call(
        functools.partial(_norm_mm_kv_kernel, tm=tm, tn=tn),
        grid=(m // tm, n // tn),
        in_specs=[pl.BlockSpec((tm, k), lambda i, j: (i, 0)),
                  pl.BlockSpec((1, k), lambda i, j: (0, 0)),
                  pl.BlockSpec((k, tn), lambda i, j: (0, j))],
        out_specs=[pl.BlockSpec((tm, tn), lambda i, j: (i, j)),
                   pl.BlockSpec((tm // SB_TK, 2, SB_TK, tn), lambda i, j: (i, 0, 0, j))],
        out_shape=[jax.ShapeDtypeStruct((m, n), F32), jax.ShapeDtypeStruct((m // SB_TK, 2, SB_TK, n), BF16)],
        scratch_shapes=[pltpu.VMEM((tm, k), BF16)],
        compiler_params=_params("parallel", "arbitrary"),
        name="norm_matmul_kv",
    )(x, g.reshape(1, k), w)


def _norm_mm_hi_kernel(x_ref, g_ref, w_ref, o_ref):
    hn = _rms(x_ref[...], g_ref[...])
    w = w_ref[...]
    h_hi = hn.astype(BF16)
    h_lo = (hn - h_hi.astype(F32)).astype(BF16)
    w_hi = w.astype(BF16)
    w_lo = (w - w_hi.astype(F32)).astype(BF16)
    o_ref[...] = _dot(h_hi, w_hi) + _dot(h_lo, w_hi) + _dot(h_hi, w_lo)


def _norm_matmul_hi(x, g, w, tm):
    m, k = x.shape
    n = w.shape[1]
    return pl.pallas_call(
        _norm_mm_hi_kernel,
        grid=(m // tm,),
        in_specs=[pl.BlockSpec((tm, k), lambda i: (i, 0)),
                  pl.BlockSpec((1, k), lambda i: (0, 0)),
                  pl.BlockSpec((k, n), lambda i: (0, 0))],
        out_specs=pl.BlockSpec((tm, n), lambda i: (i, 0)),
        out_shape=jax.ShapeDtypeStruct((m, n), F32),
        compiler_params=_params("parallel"),
        name="norm_matmul_hi",
    )(x, g.reshape(1, k), w)


def _sconv_prompt_kernel(b_ref, c_ref, h_ref, w_ref, y_ref, ns_ref, ubuf, *, tm):
    t = pl.program_id(1)
    pad = 8
    keep = SC_WIDTH - 1

    @pl.when(t == 0)
    def _():
        ubuf[0:pad, :] = jnp.zeros((pad, D_MODEL), F32)

    @pl.when(t > 0)
    def _():
        ubuf[0:pad, :] = ubuf[tm:tm + pad, :]

    u = c_ref[...] * h_ref[...]
    ubuf[pad:pad + tm, :] = u
    w = w_ref[...]
    conv = (ubuf[pad - 2:pad - 2 + tm, :] * w[0:1, :] + ubuf[pad - 1:pad - 1 + tm, :] * w[1:2, :]
            + u * w[2:3, :])
    y_ref[...] = (b_ref[...] * conv).astype(BF16)
    ns_ref[...] = ubuf[pad + tm - keep:pad + tm, :]


def _sconv_prompt(p1, w_conv, nb, t, tm):
    nt = t // tm
    blk = lambda c: pl.BlockSpec((tm, D_MODEL), lambda b, i, c=c: (b * nt + i, c))
    return pl.pallas_call(
        functools.partial(_sconv_prompt_kernel, tm=tm),
        grid=(nb, nt),
        in_specs=[blk(0), blk(1), blk(2), pl.BlockSpec((SC_WIDTH, D_MODEL), lambda b, i: (0, 0))],
        out_specs=[pl.BlockSpec((tm, D_MODEL), lambda b, i: (b * nt + i, 0)),
                   pl.BlockSpec((None, SC_WIDTH - 1, D_MODEL), lambda b, i: (b, 0, 0))],
        out_shape=[jax.ShapeDtypeStruct((nb * t, D_MODEL), BF16),
                   jax.ShapeDtypeStruct((nb, SC_WIDTH - 1, D_MODEL), F32)],
        scratch_shapes=[pltpu.VMEM((tm + 8, D_MODEL), F32)],
        compiler_params=_params("parallel", "arbitrary"),
        name="sconv_prompt",
    )(p1, p1, p1, w_conv)


def _ssd_epilogue(y, xs, z, dskip, gnorm):
    y = (y + dskip * xs) * _silu(z)
    outs = []
    for g in range(SSM_GROUPS):
        sl = slice(g * SSM_GROUP_DIM, (g + 1) * SSM_GROUP_DIM)
        outs.append(_rms(y[:, sl], gnorm[:, sl]))
    return jnp.concatenate(outs, axis=-1)


def _ssd_prompt_kernel(xa_ref, xb_ref, xc_ref, za_ref, zb_ref, dt_ref, cw_ref, cb_ref, dtb_ref,
                       alog_ref, dskip_ref, gn_ref, e_ref,
                       y_ref, ncs_ref, nst_ref, xbuf, st_ref, ybuf):
    c = pl.program_id(1)
    nc = pl.num_programs(1)
    L = SSM_CHUNK
    pad = 8
    keep = SSM_CONV - 1

    @pl.when(c == 0)
    def _():
        xbuf[0:pad, :] = jnp.zeros((pad, SSM_CONV_DIM), F32)
        st_ref[...] = jnp.zeros_like(st_ref)

    @pl.when(c > 0)
    def _():
        xbuf[0:pad, :] = xbuf[L:L + pad, :]

    xbuf[pad:pad + L, 0:D_MODEL] = xa_ref[...]
    xbuf[pad:pad + L, D_MODEL:2 * D_MODEL] = xb_ref[...]
    xbuf[pad:pad + L, 2 * D_MODEL:3 * D_MODEL] = xc_ref[...]
    cw = cw_ref[...]
    conv = cb_ref[...] + xbuf[pad:pad + L, :] * cw[keep:keep + 1, :]
    for j in range(keep):
        conv = conv + xbuf[pad - keep + j:pad - keep + j + L, :] * cw[j:j + 1, :]
    xbc = _silu(conv)
    ncs_ref[...] = xbuf[pad + L - keep:pad + L, :]

    xs = xbc[:, :SSM_DIM]
    bm = xbc[:, SSM_DIM:SSM_DIM + SSM_GN]
    cm = xbc[:, SSM_DIM + SSM_GN:]

    row = lax.broadcasted_iota(jnp.int32, (L, L), 0)
    col = lax.broadcasted_iota(jnp.int32, (L, L), 1)
    causal = col <= row
    tri = jnp.where(causal, 1.0, 0.0).astype(BF16)

    dt = _softplus(dt_ref[...] + dtb_ref[...])
    a = -jnp.exp(alog_ref[...])
    adt = a * dt
    h1, h2, h3 = _split3(adt)
    a_cs = _dot(tri, h1) + _dot(tri, h2) + _dot(tri, h3)
    a_cs_t = a_cs.T
    ea = jnp.exp(a_cs)
    dte = jnp.exp(a_cs[L - 1:L, :] - a_cs)
    e = e_ref[...]
    dt_x = _dot3(dt, e)
    ea_x = _dot3(ea, e)
    dte_x = _dot3(dte, e)
    xdt = xs * dt_x
    xdt_b = xdt.astype(BF16)
    xdec_b = (xdt * dte_x).astype(BF16)
    lane = lax.broadcasted_iota(jnp.int32, (L, LANES), 1)
    first_half = lane < SSM_HEAD_DIM

    for g in range(SSM_GROUPS):
        gs = slice(g * SSM_STATE, (g + 1) * SSM_STATE)
        gd = slice(g * SSM_GROUP_DIM, (g + 1) * SSM_GROUP_DIM)
        b_g = bm[:, gs]
        c_b = cm[:, gs].astype(BF16)
        cb = _dot_nt(c_b, b_g.astype(BF16))
        st_g = st_ref[g]
        ybuf[:, gd] = _dot(c_b, st_g.astype(BF16)) * ea_x[:, gd]
        for pr in range(SSM_GROUP_DIM // LANES):
            res = []
            for half in range(2):
                h = g * (SSM_HEADS // SSM_GROUPS) + 2 * pr + half
                seg = jnp.exp(jnp.where(causal, a_cs[:, h:h + 1] - a_cs_t[h:h + 1, :], -jnp.inf))
                res.append((cb * seg).astype(BF16))
            ps = slice(g * SSM_GROUP_DIM + pr * LANES, g * SSM_GROUP_DIM + (pr + 1) * LANES)
            x2 = xdt_b[:, ps]
            ybuf[:, ps] += jnp.where(first_half, _dot(res[0], x2), _dot(res[1], x2))
        st_ref[g] = st_g * ea_x[L - 1:L, gd] + _dot(b_g.T.astype(BF16), xdec_b[:, gd])

    z = jnp.concatenate([za_ref[...], zb_ref[...]], axis=-1)
    y_ref[...] = _ssd_epilogue(ybuf[...], xs, z, dskip_ref[...], gn_ref[...]).astype(BF16)

    @pl.when(c == nc - 1)
    def _():
        for g in range(SSM_GROUPS):
            nst_ref[g * SSM_GROUP_DIM:(g + 1) * SSM_GROUP_DIM, :] = st_ref[g].T


def _ssd_prompt(p1, pdt, cw, cb, dtb, alog, dskip_x, gn, e, nb, t):
    L = SSM_CHUNK
    nc = t // L
    blk = lambda c: pl.BlockSpec((L, D_MODEL), lambda b, i, c=c: (b * nc + i, c))
    full = lambda r, n: pl.BlockSpec((r, n), lambda b, i: (0, 0))
    return pl.pallas_call(
        _ssd_prompt_kernel,
        grid=(nb, nc),
        in_specs=[blk(5), blk(6), blk(7), blk(3), blk(4),
                  pl.BlockSpec((L, LANES), lambda b, i: (b * nc + i, 0)),
                  full(SSM_CONV, SSM_CONV_DIM), full(1, SSM_CONV_DIM), full(1, LANES), full(1, LANES),
                  full(1, SSM_DIM), full(1, SSM_DIM), full(LANES, SSM_DIM)],
        out_specs=[pl.BlockSpec((L, SSM_DIM), lambda b, i: (b * nc + i, 0)),
                   pl.BlockSpec((None, SSM_CONV - 1, SSM_CONV_DIM), lambda b, i: (b, 0, 0)),
                   pl.BlockSpec((None, SSM_DIM, SSM_STATE), lambda b, i: (b, 0, 0))],
        out_shape=[jax.ShapeDtypeStruct((nb * t, SSM_DIM), BF16),
                   jax.ShapeDtypeStruct((nb, SSM_CONV - 1, SSM_CONV_DIM), F32),
                   jax.ShapeDtypeStruct((nb, SSM_DIM, SSM_STATE), F32)],
        scratch_shapes=[pltpu.VMEM((L + 8, SSM_CONV_DIM), F32),
                        pltpu.VMEM((SSM_GROUPS, SSM_STATE, SSM_GROUP_DIM), F32),
                        pltpu.VMEM((L, SSM_DIM), F32)],
        compiler_params=_params("parallel", "arbitrary"),
        name="ssd_prompt",
    )(p1, p1, p1, p1, p1, pdt, cw, cb, dtb, alog, dskip_x, gn, e)


def _softplus2(z):
    neg_abs = pltpu.bitcast(pltpu.bitcast(z, jnp.int32) | jnp.int32(-2 ** 31), F32)
    return jnp.maximum(z, 0.0) + jnp.log2(1.0 + jnp.exp2(neg_abs))


def _sb_prompt_kernel(bias_ref, q_ref, k_ref, v_ref, o_ref, acc_ref, tot_ref, z_ref, w_ref):
    gq = pl.program_id(1)
    i = pl.program_id(2)
    tq, tk, grp = SB_TQ, SB_TK, SB_GROUP
    lane2 = lax.broadcasted_iota(jnp.int32, (1, 2 * tk), 1)
    r2 = lax.broadcasted_iota(jnp.int32, (2 * tk, 2 * tk), 0)
    c2 = lax.broadcasted_iota(jnp.int32, (2 * tk, 2 * tk), 1)
    same = (r2 < tk) == (c2 < tk)
    m_tot = jnp.where(same, 1.0, 0.0).astype(BF16)
    m_cum = jnp.where(same & (r2 >= c2), 1.0, 0.0).astype(BF16)
    acc_ref[...] = jnp.zeros_like(acc_ref)
    tot_ref[...] = jnp.zeros_like(tot_ref)
    q_pos = i * tq + lax.broadcasted_iota(jnp.int32, (tq, 2 * tk), 0)
    k_off = lax.broadcasted_iota(jnp.int32, (tq, 2 * tk), 1) % tk
    cols = [slice(pp * LANES, (pp + 1) * LANES) for pp in range(grp)]
    biases = [jnp.where(lane2 < tk, bias_ref[2 * (gq * grp + pp)], bias_ref[2 * (gq * grp + pp) + 1]) * LOG2E
              for pp in range(grp)]

    def logits(jb, slot):
        jb = jnp.maximum(jb, 0)
        for pp in range(grp):
            z_ref[slot, pp] = _dot_nt(q_ref[:, cols[pp]], k_ref[jb, :, :, cols[pp]].reshape(2 * tk, LANES)) + biases[pp]

    def weights(jb, slot, masked):
        if masked:
            valid = (jb * tk + k_off) < q_pos
        for pp in range(grp):
            z = z_ref[slot, pp]
            sp = _softplus2(z)
            if masked:
                sp = jnp.where(valid, sp, 0.0)
            sp_b = sp.astype(BF16)
            later = tot_ref[pp] + _dot(sp_b, m_cum)
            w = jnp.exp2(z - later)
            if masked:
                w = jnp.where(valid, w, 0.0)
            w_ref[slot, pp] = w.astype(BF16)
            tot_ref[pp] += _dot(sp_b, m_tot)

    def values(jb, slot):
        for pp in range(grp):
            acc_ref[:, cols[pp]] += _dot(w_ref[slot, pp], v_ref[jb, :, :, cols[pp]].reshape(2 * tk, LANES))

    assert tq == 2 * tk
    d0 = 2 * i + 1
    logits(d0, 0)
    logits(d0 - 1, 1)
    weights(d0, 0, True)
    logits(d0 - 2, 0)
    weights(d0 - 1, 1, True)
    values(d0, 0)

    def body(t, carry):
        a = d0 - 2 - 2 * t
        logits(a - 1, 1)
        weights(a, 0, False)
        values(a + 1, 1)
        logits(a - 2, 0)
        weights(a - 1, 1, False)
        values(a, 0)
        return carry

    lax.fori_loop(0, i, body, 0)
    values(0, 1)
    o_ref[...] = acc_ref[...].astype(BF16)


def _sb_prompt(q16, k16, v16, sb_bias, nb, t):
    nq = t // SB_TQ
    nk = t // SB_TK
    gw = SB_GROUP * LANES
    ng = D_MODEL // gw
    return pl.pallas_call(
        _sb_prompt_kernel,
        grid=(nb, ng, nq),
        in_specs=[pl.BlockSpec(memory_space=pltpu.SMEM),
                  pl.BlockSpec((SB_TQ, gw), lambda b, g, i: (b * nq + i, g)),
                  pl.BlockSpec((nk, 2, SB_TK, gw), lambda b, g, i: (b, 0, 0, g)),
                  pl.BlockSpec((nk, 2, SB_TK, gw), lambda b, g, i: (b, 0, 0, g))],
        out_specs=pl.BlockSpec((SB_TQ, gw), lambda b, g, i: (b * nq + i, g)),
        out_shape=jax.ShapeDtypeStruct((nb * t, D_MODEL), BF16),
        scratch_shapes=[pltpu.VMEM((SB_TQ, gw), F32), pltpu.VMEM((SB_GROUP, SB_TQ, 2 * SB_TK), F32),
                        pltpu.VMEM((2, SB_GROUP, SB_TQ, 2 * SB_TK), F32),
                        pltpu.VMEM((2, SB_GROUP, SB_TQ, 2 * SB_TK), BF16)],
        compiler_params=_params("parallel", "parallel", "arbitrary"),
        name="sb_prompt",
    )(sb_bias, q16, k16, v16)


def _sb_decode_kernel(pt_ref, bias_ref, q_ref, kn_ref, vn_ref, *rest, pps, n_pages):
    k_refs = rest[:pps]
    v_refs = rest[pps:2 * pps]
    o_ref, tot_ref, acc_ref = rest[2 * pps:]
    s = pl.program_id(1)
    page, nh = SB_TK, SB_HEADS
    flat = page * nh
    nt = flat // LANES
    lane = lax.broadcasted_iota(jnp.int32, (nh, LANES), 1)
    row = lax.broadcasted_iota(jnp.int32, (nh, LANES), 0)
    own = lane % nh == row
    l1 = lax.broadcasted_iota(jnp.int32, (LANES, LANES), 0)
    l2 = lax.broadcasted_iota(jnp.int32, (LANES, LANES), 1)
    same_head = l1 % nh == l2 % nh
    m_row = jnp.where(same_head & (l1 // nh >= l2 // nh), 1.0, 0.0).astype(BF16)
    m_head = jnp.where(same_head, 1.0, 0.0).astype(BF16)
    ra = lax.broadcasted_iota(jnp.int32, (nt, nt), 0)
    rb = lax.broadcasted_iota(jnp.int32, (nt, nt), 1)
    m_below = jnp.where(rb > ra, 1.0, 0.0).astype(BF16)
    m_all = jnp.ones((nt, nt), BF16)
    gather = jnp.where(lax.broadcasted_iota(jnp.int32, (nt, nt * nh), 1) // nh
                       == lax.broadcasted_iota(jnp.int32, (nt, nt * nh), 0), 1.0, 0.0).astype(BF16)
    bias = bias_ref[...]
    q = q_ref[...] * SB_SCALE
    q_b = q.astype(BF16)

    def by_lane_head(col):
        spread = jnp.where(own, jnp.broadcast_to(col, (nh, LANES)), 0.0)
        return jnp.broadcast_to(jnp.sum(spread, axis=0, keepdims=True), (nt, LANES))

    @pl.when(s == 0)
    def _():
        past = n_pages * page
        valid = jnp.full((nh, 1), past, jnp.int32) < jnp.full((nh, 1), past, jnp.int32)
        z_new = jnp.sum(q * kn_ref[...], axis=-1, keepdims=True) + jnp.sum(
            jnp.where(own, bias, 0.0), axis=-1, keepdims=True)
        sp_new = jnp.where(valid, _softplus(z_new), 0.0)
        w_new = jnp.where(valid, jnp.exp(z_new - sp_new), 0.0)
        tot_ref[...] = by_lane_head(sp_new)
        acc_ref[...] = w_new * vn_ref[...]

    zs, rows, below, totals = [], [], [], []
    for r in range(pps):
        k_b = k_refs[r][...].reshape(flat, SB_HEAD_DIM).astype(BF16)
        cross = _dot_nt(q_b, k_b)
        tiles = jnp.concatenate([jnp.where(own, cross[:, a * LANES:(a + 1) * LANES], 0.0) for a in range(nt)],
                                axis=0)
        zs.append(bias + _dot2_left(gather, tiles))
    for r in range(pps):
        sp = _softplus(zs[r])
        rows.append(_dot2_right(sp, m_row))
        per_row = _dot2_right(sp, m_head)
        below.append(_dot2_left(m_below, per_row))
        totals.append(_dot2_left(m_all, per_row))
    tot = tot_ref[...]
    o = acc_ref[...]
    for r in range(pps):
        w = jnp.exp(zs[r] - (tot + below[r] + rows[r]))
        tot = tot + totals[r]
        w_x = jnp.concatenate([jnp.where(own, jnp.broadcast_to(w[a:a + 1, :], (nh, LANES)), 0.0)
                               for a in range(nt)], axis=1).astype(BF16)
        o = o + _dot(w_x, v_refs[r][...].reshape(flat, SB_HEAD_DIM).astype(BF16))
    tot_ref[...] = tot
    acc_ref[...] = o

    @pl.when(s == pl.num_programs(1) - 1)
    def _():
        o_ref[...] = acc_ref[...]


def _sb_decode(q, k_new, v_new, sb_bias_tiled, cache_k, cache_v, page_table, layer, pps):
    ns = q.shape[0]
    n_pages = page_table.shape[1]
    n_steps = n_pages // pps
    nt = SB_TK * SB_HEADS // LANES
    tok = pl.BlockSpec((None, SB_HEADS, SB_HEAD_DIM), lambda n, s, pt: (n, 0, 0))

    def page_spec(r):
        def imap(n, s, pt):
            return (layer, pt[n * n_pages + (n_pages - 1 - (s * pps + r))], 0, 0, 0)
        return pl.BlockSpec((None, None, SB_TK, SB_HEADS, SB_HEAD_DIM), imap)

    specs = [pl.BlockSpec((nt, LANES), lambda n, s, pt: (0, 0)), tok, tok, tok]
    specs += [page_spec(r) for r in range(pps)] * 2
    grid_spec = pltpu.PrefetchScalarGridSpec(
        num_scalar_prefetch=1,
        grid=(ns, n_steps),
        in_specs=specs,
        out_specs=tok,
        scratch_shapes=[pltpu.VMEM((nt, LANES), F32), pltpu.VMEM((SB_HEADS, SB_HEAD_DIM), F32)])
    return pl.pallas_call(
        functools.partial(_sb_decode_kernel, pps=pps, n_pages=n_pages),
        grid_spec=grid_spec,
        out_shape=jax.ShapeDtypeStruct((ns, SB_HEADS, SB_HEAD_DIM), F32),
        compiler_params=_params("parallel", "arbitrary"),
        name="sb_decode",
    )(page_table.reshape(-1), sb_bias_tiled, q, k_new, v_new, *([cache_k] * pps), *([cache_v] * pps))


def _sample_pre_kernel(p1_ref, dt_ref, ssc_ref, sxc_ref, scw_ref, cw_ref, cb_ref, dtb_ref, alog_ref, e_ref,
                       ya_ref, nsc_ref, nxc_ref, xs_ref, b_ref, c_ref, dtx_ref, dec_ref):
    d = D_MODEL
    p1 = p1_ref[...]
    u = p1[:, d:2 * d] * p1[:, 2 * d:3 * d]
    ssc = ssc_ref[...]
    scw = scw_ref[...]
    conv_a = ssc[:, 0:d] * scw[0:1, :] + ssc[:, d:2 * d] * scw[1:2, :] + u * scw[2:3, :]
    ya_ref[...] = (p1[:, 0:d] * conv_a).astype(BF16)
    nsc_ref[:, 0:d] = ssc[:, d:2 * d]
    nsc_ref[:, d:2 * d] = u
    cd = SSM_CONV_DIM
    xbc = p1[:, 3 * d + SSM_DIM:3 * d + SSM_DIM + cd]
    sxc = sxc_ref[...]
    cw = cw_ref[...]
    conv = cb_ref[...] + xbc * cw[SSM_CONV - 1:SSM_CONV, :]
    for j in range(SSM_CONV - 1):
        conv = conv + sxc[:, j * cd:(j + 1) * cd] * cw[j:j + 1, :]
        if j > 0:
            nxc_ref[:, (j - 1) * cd:j * cd] = sxc[:, j * cd:(j + 1) * cd]
    nxc_ref[:, (SSM_CONV - 2) * cd:(SSM_CONV - 1) * cd] = xbc
    act = _silu(conv)
    xs = act[:, :SSM_DIM]
    xs_ref[...] = xs
    b_ref[...] = act[:, SSM_DIM:SSM_DIM + SSM_GN]
    c_ref[...] = act[:, SSM_DIM + SSM_GN:]
    dt = _softplus(dt_ref[...] + dtb_ref[...])
    dec = jnp.exp(-jnp.exp(alog_ref[...]) * dt)
    e = e_ref[...]
    dtx_ref[...] = xs * _dot3(dt, e)
    dec_ref[...] = _dot3(dec, e)


def _sample_pre(p1s, pdts, ssc, sxc, scw, cw, cb, dtb, alog, e):
    ns = p1s.shape[0]
    f = lambda n: jax.ShapeDtypeStruct((ns, n), F32)
    return pl.pallas_call(
        _sample_pre_kernel,
        out_shape=[jax.ShapeDtypeStruct((ns, D_MODEL), BF16), f((SC_WIDTH - 1) * D_MODEL),
                   f((SSM_CONV - 1) * SSM_CONV_DIM), f(SSM_DIM), f(SSM_GN), f(SSM_GN), f(SSM_DIM), f(SSM_DIM)],
        compiler_params=pltpu.CompilerParams(vmem_limit_bytes=VMEM_LIMIT),
        name="sample_pre",
    )(p1s, pdts, ssc, sxc, scw, cw, cb, dtb, alog, e)


def _sample_state_kernel(st_ref, dtx_ref, dec_ref, b_ref, c_ref, nst_ref, y_ref):
    nblk = SSM_DIM // LANES
    zpad = jnp.zeros((LANES - nblk, LANES), F32)
    dtx_t = jnp.concatenate([dtx_ref[...], zpad], axis=0).T
    dec_t = jnp.concatenate([dec_ref[...], zpad], axis=0).T
    b = b_ref[...]
    c = c_ref[...]
    lane = lax.broadcasted_iota(jnp.int32, (LANES, LANES), 1)
    ycols = jnp.zeros((LANES, LANES), F32)
    for i in range(nblk):
        g = i // (nblk // SSM_GROUPS)
        rows = slice(i * LANES, (i + 1) * LANES)
        new = st_ref[rows, :] * dec_t[:, i:i + 1] + dtx_t[:, i:i + 1] * b[g:g + 1, :]
        nst_ref[rows, :] = new
        ycol = jnp.sum(new * c[g:g + 1, :], axis=-1, keepdims=True)
        ycols = jnp.where(lane == i, ycol, ycols)
    y_ref[...] = ycols.T[0:nblk, :]


def _sample_state(state_all, layer, dtx, dec, b, c):
    ns = state_all.shape[1]
    nblk = SSM_DIM // LANES
    vec = lambda r: pl.BlockSpec((None, r, LANES), lambda n: (n, 0, 0))
    st_spec = pl.BlockSpec((None, SSM_DIM, SSM_STATE), lambda n: (n, 0, 0))
    st_in = pl.BlockSpec((None, None, SSM_DIM, SSM_STATE), lambda n: (layer, n, 0, 0))
    nst, y = pl.pallas_call(
        _sample_state_kernel,
        grid=(ns,),
        in_specs=[st_in, vec(nblk), vec(nblk), vec(SSM_GROUPS), vec(SSM_GROUPS)],
        out_specs=[st_spec, vec(nblk)],
        out_shape=[jax.ShapeDtypeStruct((ns, SSM_DIM, SSM_STATE), F32),
                   jax.ShapeDtypeStruct((ns, nblk, LANES), F32)],
        compiler_params=_params("parallel"),
        name="sample_state",
    )(state_all, dtx.reshape(ns, nblk, LANES), dec.reshape(ns, nblk, LANES),
      b.reshape(ns, SSM_GROUPS, SSM_STATE), c.reshape(ns, SSM_GROUPS, SSM_STATE))
    return nst, y.reshape(ns, SSM_DIM)


def _sample_post_kernel(y_ref, xs_ref, p1_ref, dskip_ref, gn_ref, o_ref):
    z = p1_ref[:, 3 * D_MODEL:3 * D_MODEL + SSM_DIM]
    o_ref[...] = _ssd_epilogue(y_ref[...], xs_ref[...], z, dskip_ref[...], gn_ref[...]).astype(BF16)


def _sample_post(y, xs, p1s, dskip_x, gn):
    return pl.pallas_call(
        _sample_post_kernel,
        out_shape=jax.ShapeDtypeStruct(y.shape, BF16),
        compiler_params=pltpu.CompilerParams(vmem_limit_bytes=VMEM_LIMIT),
        name="sample_post",
    )(y, xs, p1s, dskip_x, gn)


def _merge_kernel(x_ref, ya_ref, yb_ref, yc_ref, g0_ref, g1_ref, g2_ref, wa_ref, wb_ref, wc_ref, wo_ref,
                  gx_ref, wq_ref, x1_ref, q_ref):
    merged = (jax.nn.sigmoid(g0_ref[...]) * _dot(ya_ref[...], wa_ref[...])
              + jax.nn.sigmoid(g1_ref[...]) * _dot(yb_ref[...], wb_ref[...])
              + jax.nn.sigmoid(g2_ref[...]) * _dot(yc_ref[...], wc_ref[...]))
    x1 = x_ref[...] + _dot(merged.astype(BF16), wo_ref[...])
    x1_ref[...] = x1
    hn = _rms(x1, gx_ref[...]).astype(BF16)
    q_ref[...] = (_dot(hn, wq_ref[...]) * X_SCALE).astype(BF16)


def _merge(x, ya, yb, yc, pg, wa, wb, wc, wo, gx, wq, tm):
    m = x.shape[0]
    d = D_MODEL
    rows = lambda n, c=0: pl.BlockSpec((tm, n), lambda i, c=c: (i, c))
    return pl.pallas_call(
        _merge_kernel,
        grid=(m // tm,),
        in_specs=[rows(d), rows(d), rows(SSM_DIM), rows(d), rows(d, 0), rows(d, 1), rows(d, 2),
                  _resident((d, d)), _resident((SSM_DIM, d)), _resident((d, d)), _resident((d, d)),
                  _resident((1, d)), _resident((d, d))],
        out_specs=[rows(d), rows(d)],
        out_shape=[jax.ShapeDtypeStruct((m, d), F32), jax.ShapeDtypeStruct((m, d), BF16)],
        compiler_params=_params("parallel"),
        name="merge",
    )(x, ya, yb, yc, pg, pg, pg, wa, wb, wc, wo, gx.reshape(1, d), wq)


def _mem_attn_kernel(q_ref, k_ref, v_ref, o_ref):
    q = q_ref[...]
    outs = []
    for h in range(X_HEADS):
        hs = slice(h * X_HEAD_DIM, (h + 1) * X_HEAD_DIM)
        s = _dot_nt(q[:, hs], k_ref[:, hs].astype(BF16))
        e = jnp.exp(s - jnp.max(s, axis=-1, keepdims=True))
        prob = e / jnp.sum(e, axis=-1, keepdims=True)
        outs.append(_dot(prob.astype(BF16), v_ref[:, hs].astype(BF16)))
    o_ref[...] = jnp.concatenate(outs, axis=-1).astype(BF16)


def _mem_attn(q, mem_k, mem_v, layer, tm):
    nb, t, d = q.shape
    kv = pl.BlockSpec((None, None, N_MEM, d), lambda b, i: (layer, b, 0, 0))
    qs = pl.BlockSpec((None, tm, d), lambda b, i: (b, i, 0))
    return pl.pallas_call(
        _mem_attn_kernel,
        grid=(nb, t // tm),
        in_specs=[qs, kv, kv],
        out_specs=qs,
        out_shape=jax.ShapeDtypeStruct((nb, t, d), BF16),
        compiler_params=_params("parallel", "arbitrary"),
        name="mem_attn",
    )(q, mem_k, mem_v)


def _mlp_kernel(x1_ref, at_ref, wxo_ref, gm_ref, w1_ref, w2_ref, gf_ref, o_ref, *, final):
    x2 = x1_ref[...] + _dot(at_ref[...], wxo_ref[...])
    hn = _rms(x2, gm_ref[...]).astype(BF16)
    acc = x2
    fc = 1024
    for c in range(D_FF // fc):
        h = jnp.maximum(_dot(hn, w1_ref[:, c * fc:(c + 1) * fc]), 0.0)
        acc = acc + _dot((h * h).astype(BF16), w2_ref[c * fc:(c + 1) * fc, :])
    o_ref[...] = _rms(acc, gf_ref[...]) if final else acc


def _mlp(x1, attn, wxo, gm, w1, w2, gf, tm, final):
    m, d = x1.shape
    rows = pl.BlockSpec((tm, d), lambda i: (i, 0))
    return pl.pallas_call(
        functools.partial(_mlp_kernel, final=final),
        grid=(m // tm,),
        in_specs=[rows, rows, _resident((d, d)), _resident((1, d)), _resident((d, D_FF)),
                  _resident((D_FF, d)), _resident((1, d))],
        out_specs=rows,
        out_shape=jax.ShapeDtypeStruct((m, d), F32),
        compiler_params=_params("parallel"),
        name="mlp",
    )(x1, attn, wxo, gm.reshape(1, d), w1, w2, gf.reshape(1, d))


def _pad_lanes(v):
    return jnp.pad(v.astype(F32), (0, LANES - v.shape[0])).reshape(1, LANES)


def kernel(x_prompt, x_sample, cache_sb_k, cache_sb_v, page_table, state_sconv, state_ssm_conv, state_ssm, cache_mem_k, cache_mem_v, mem_prompt, g_mix, w_in, w_sc_conv, w_ssm_conv, b_ssm_conv, dt_bias, a_log, d_skip, g_ssm_norm, sb_bias, w_br_sc, w_br_ssm, w_br_sb, w_out, g_x, g_mem, w_xq, w_xk, w_xv, w_xo, g_mlp, w_ff1, w_ff2, g_final):
    nb, t, d = x_prompt.shape
    ns = x_sample.shape[0]
    depth = w_in.shape[0]
    n_pages = page_table.shape[1]
    mp = nb * t
    tm_p = min(1024, t)
    pps = 4 if n_pages % 4 == 0 else n_pages

    xp = x_prompt.reshape(mp, d)
    xs = x_sample.reshape(ns, d)
    mem2 = mem_prompt.reshape(nb * N_MEM, d)
    st_all = state_ssm.reshape(depth, ns, SSM_DIM, SSM_STATE)
    cmk = cache_mem_k.reshape(depth, ns, N_MEM, d)
    cmv = cache_mem_v.reshape(depth, ns, N_MEM, d)
    head_of_channel = jnp.arange(SSM_DIM) // SSM_HEAD_DIM
    expand = (jnp.arange(LANES)[:, None] == head_of_channel[None, :]).astype(BF16)

    outs = {k: [] for k in ("p_k", "p_v", "p_sc", "p_ssc", "p_ssm", "p_mk", "p_mv",
                            "s_k", "s_v", "s_sc", "s_ssc", "s_ssm")}
    for l in range(depth):
        final = l == depth - 1
        w1 = w_in[l, :, :OFF_DT].astype(BF16)
        wqkv = w_in[l, :, OFF_Q:OFF_GATES].astype(BF16)
        wq_sb, wk_sb, wv_sb = wqkv[:, :d], wqkv[:, d:2 * d], wqkv[:, 2 * d:]
        wg = w_in[l, :, OFF_GATES:].astype(BF16)
        wdt = jnp.pad(w_in[l, :, OFF_DT:OFF_Q], ((0, 0), (0, LANES - SSM_HEADS)))
        wa, wb, wc, wo = (w[l].astype(BF16) for w in (w_br_sc, w_br_ssm, w_br_sb, w_out))
        wq, wxo, wf1, wf2 = (w[l].astype(BF16) for w in (w_xq, w_xo, w_ff1, w_ff2))
        cb = b_ssm_conv[l].reshape(1, SSM_CONV_DIM)
        dtb = _pad_lanes(dt_bias[l])
        alog = _pad_lanes(a_log[l])
        dskip_x = d_skip[l].astype(F32)[head_of_channel].reshape(1, SSM_DIM)
        gn = g_ssm_norm[l].reshape(1, SSM_DIM)
        bias_tiled = jnp.broadcast_to(jnp.tile(sb_bias[l].astype(F32), LANES // SB_HEADS)[None, :],
                                      (SB_TK * SB_HEADS // LANES, LANES))

        mk = _norm_matmul(mem2, g_mem[l], w_xk[l].astype(BF16), tm=min(1024, nb * N_MEM), tn=512)
        mv = _norm_matmul(mem2, g_mem[l], w_xv[l].astype(BF16), tm=min(1024, nb * N_MEM), tn=512)
        p1 = _norm_matmul(xp, g_mix[l], w1, tm=tm_p, tn=512)
        q16 = _norm_matmul(xp, g_mix[l], wq_sb, tm=tm_p, tn=512, out_dtype=BF16, out_scale=SB_SCALE * LOG2E)
        k32, k16 = _norm_matmul_kv(xp, g_mix[l], wk_sb, tm=tm_p, tn=512)
        v32, v16 = _norm_matmul_kv(xp, g_mix[l], wv_sb, tm=tm_p, tn=512)
        pg = _norm_matmul(xp, g_mix[l], wg, tm=tm_p, tn=512)
        pdt = _norm_matmul_hi(xp, g_mix[l], wdt, tm=tm_p)
        ya, p_sc = _sconv_prompt(p1, w_sc_conv[l], nb, t, tm=min(512, t))
        yb, p_ssc, p_ssm = _ssd_prompt(p1, pdt, w_ssm_conv[l], cb, dtb, alog, dskip_x, gn, expand, nb, t)
        yc = _sb_prompt(q16, k16, v16, sb_bias[l].astype(F32), nb, t)
        x1, qx = _merge(xp, ya, yb, yc, pg, wa, wb, wc, wo, g_x[l], wq, tm=256)
        attn = _mem_attn(qx.reshape(nb, t, d), mk.reshape(1, nb, N_MEM, d), mv.reshape(1, nb, N_MEM, d), 0,
                         tm=min(512, t))
        xp = _mlp(x1, attn.reshape(mp, d), wxo, g_mlp[l], wf1, wf2, g_final, tm=256, final=final)
        outs["p_k"].append(k32.reshape(nb, t, SB_HEADS, SB_HEAD_DIM))
        outs["p_v"].append(v32.reshape(nb, t, SB_HEADS, SB_HEAD_DIM))
        outs["p_sc"].append(p_sc)
        outs["p_ssc"].append(p_ssc)
        outs["p_ssm"].append(p_ssm.reshape(nb, SSM_HEADS, SSM_HEAD_DIM, SSM_STATE))
        outs["p_mk"].append(mk.reshape(nb, N_MEM, X_HEADS, X_HEAD_DIM))
        outs["p_mv"].append(mv.reshape(nb, N_MEM, X_HEADS, X_HEAD_DIM))

        p1s = _norm_matmul(xs, g_mix[l], w1, tm=ns, tn=512)
        qkv_s = _norm_matmul(xs, g_mix[l], wqkv, tm=ns, tn=512)
        pg_s = _norm_matmul(xs, g_mix[l], wg, tm=ns, tn=512)
        q_s, k_s, v_s = (qkv_s[:, c * d:(c + 1) * d].reshape(ns, SB_HEADS, SB_HEAD_DIM) for c in range(3))
        pdts = _norm_matmul_hi(xs, g_mix[l], wdt, tm=ns)
        ya_s, s_sc, s_ssc, xs_c, b_c, c_c, dtx, dec = _sample_pre(
            p1s, pdts, state_sconv[l].reshape(ns, -1), state_ssm_conv[l].reshape(ns, -1),
            w_sc_conv[l], w_ssm_conv[l], cb, dtb, alog, expand)
        s_ssm, y_s = _sample_state(st_all, l, dtx, dec, b_c, c_c)
        yb_s = _sample_post(y_s, xs_c, p1s, dskip_x, gn)
        yc_s = _sb_decode(q_s, k_s, v_s, bias_tiled, cache_sb_k, cache_sb_v, page_table, l, pps)
        yc_s = yc_s.reshape(ns, d).astype(BF16)
        x1s, qxs = _merge(xs, ya_s, yb_s, yc_s, pg_s, wa, wb, wc, wo, g_x[l], wq, tm=ns)
        attn_s = _mem_attn(qxs.reshape(ns, 1, d), cmk, cmv, l, tm=1)
        xs = _mlp(x1s, attn_s.reshape(ns, d), wxo, g_mlp[l], wf1, wf2, g_final, tm=ns, final=final)
        outs["s_k"].append(k_s.reshape(ns, 1, SB_HEADS, SB_HEAD_DIM))
        outs["s_v"].append(v_s.reshape(ns, 1, SB_HEADS, SB_HEAD_DIM))
        outs["s_sc"].append(s_sc.reshape(ns, SC_WIDTH - 1, D_MODEL))
        outs["s_ssc"].append(s_ssc.reshape(ns, SSM_CONV - 1, SSM_CONV_DIM))
        outs["s_ssm"].append(s_ssm.reshape(ns, SSM_HEADS, SSM_HEAD_DIM, SSM_STATE))

    st = {k: jnp.stack(v) for k, v in outs.items()}
    return (xp.reshape(nb, t, d), xs.reshape(ns, 1, d),
            st["p_k"], st["p_v"], st["p_sc"], st["p_ssc"], st["p_ssm"], st["p_mk"], st["p_mv"],
            st["s_k"], st["s_v"], st["s_sc"], st["s_ssc"], st["s_ssm"])
```

```python
import functools

import jax
import jax.numpy as jnp
from jax import lax
from jax.experimental import pallas as pl
from jax.experimental.pallas import tpu as pltpu

F32 = jnp.float32
BF16 = jnp.bfloat16

D_MODEL = 1024
SC_WIDTH = 3
SSM_DIM = 2048
SSM_HEAD_DIM = 64
SSM_HEADS = 32
SSM_GROUPS = 4
SSM_STATE = 128
SSM_GN = SSM_GROUPS * SSM_STATE
SSM_GROUP_DIM = SSM_DIM // SSM_GROUPS
SSM_CONV = 4
SSM_CONV_DIM = SSM_DIM + 2 * SSM_GN
SSM_CHUNK = 128
SB_HEADS = 16
SB_HEAD_DIM = 64
SB_SCALE = SB_HEAD_DIM ** -0.5
SB_TQ = 256
SB_TK = 128
SB_GROUP = 4
LOG2E = 1.4426950408889634
X_HEADS = 4
X_HEAD_DIM = 256
X_SCALE = X_HEAD_DIM ** -0.5
N_MEM = 256
D_FF = 4096
EPS = 1e-6
LANES = 128
OFF_DT = 3 * D_MODEL + SSM_DIM + SSM_CONV_DIM
OFF_Q = OFF_DT + SSM_HEADS
OFF_GATES = OFF_Q + 3 * D_MODEL
VMEM_LIMIT = 56 * 1024 * 1024


def _params(*sem):
    return pltpu.CompilerParams(dimension_semantics=sem, vmem_limit_bytes=VMEM_LIMIT)


def _resident(shape):
    nd = len(shape)
    return pl.BlockSpec(shape, lambda *_: (0,) * nd, pipeline_mode=pl.Buffered(1))


def _rms(x, g):
    return x * lax.rsqrt(jnp.mean(x * x, axis=-1, keepdims=True) + EPS) * g


def _softplus(x):
    return jnp.maximum(x, 0.0) + jnp.log1p(jnp.exp(-jnp.abs(x)))


def _silu(x):
    return x * jax.nn.sigmoid(x)


def _split3(x):
    h1 = x.astype(BF16)
    r1 = x - h1.astype(F32)
    h2 = r1.astype(BF16)
    h3 = (r1 - h2.astype(F32)).astype(BF16)
    return h1, h2, h3


def _split2(x):
    hi = x.astype(BF16)
    return hi, (x - hi.astype(F32)).astype(BF16)


def _dot(a, b):
    return jnp.dot(a, b, preferred_element_type=F32)


def _dot_nt(a, b):
    return lax.dot_general(a, b, (((1,), (1,)), ((), ())), preferred_element_type=F32)


def _dot3(x, m):
    h1, h2, h3 = _split3(x)
    return _dot(h1, m) + _dot(h2, m) + _dot(h3, m)


def _dot2_left(m, x):
    hi, lo = _split2(x)
    return _dot(m, hi) + _dot(m, lo)


def _dot2_right(x, m):
    hi, lo = _split2(x)
    return _dot(hi, m) + _dot(lo, m)


def _norm_mm_kernel(x_ref, g_ref, w_ref, o_ref, hn_ref, *, out_scale):
    @pl.when(pl.program_id(1) == 0)
    def _():
        hn_ref[...] = _rms(x_ref[...], g_ref[...]).astype(BF16)

    r = _dot(hn_ref[...], w_ref[...])
    if out_scale is not None:
        r = r * out_scale
    o_ref[...] = r.astype(o_ref.dtype)


def _norm_matmul(x, g, w, tm, tn, out_dtype=F32, out_scale=None):
    m, k = x.shape
    n = w.shape[1]
    return pl.pallas_call(
        functools.partial(_norm_mm_kernel, out_scale=out_scale),
        grid=(m // tm, n // tn),
        in_specs=[pl.BlockSpec((tm, k), lambda i, j: (i, 0)),
                  pl.BlockSpec((1, k), lambda i, j: (0, 0)),
                  pl.BlockSpec((k, tn), lambda i, j: (0, j))],
        out_specs=pl.BlockSpec((tm, tn), lambda i, j: (i, j)),
        out_shape=jax.ShapeDtypeStruct((m, n), out_dtype),
        scratch_shapes=[pltpu.VMEM((tm, k), BF16)],
        compiler_params=_params("parallel", "arbitrary"),
        name="norm_matmul",
    )(x, g.reshape(1, k), w)


def _norm_mm_kv_kernel(x_ref, g_ref, w_ref, o32_ref, o16_ref, hn_ref, *, tm, tn):
    @pl.when(pl.program_id(1) == 0)
    def _():
        hn_ref[...] = _rms(x_ref[...], g_ref[...]).astype(BF16)

    r = _dot(hn_ref[...], w_ref[...])
    o32_ref[...] = r
    even = (lax.broadcasted_iota(jnp.int32, (1, tn), 1) // SB_HEAD_DIM) % 2 == 0
    for a in range(tm // SB_TK):
        blk = r[a * SB_TK:(a + 1) * SB_TK, :]
        o16_ref[a, 0] = jnp.where(even, blk, 0.0).astype(BF16)
        o16_ref[a, 1] = jnp.where(even, 0.0, blk).astype(BF16)


def _norm_matmul_kv(x, g, w, tm, tn):
    m, k = x.shape
    n = w.shape[1]
    return pl.pallas_call(
        functools.partial(_norm_mm_kv_kernel, tm=tm, tn=tn),
        grid=(m // tm, n // tn),
        in_specs=[pl.BlockSpec((tm, k), lambda i, j: (i, 0)),
                  pl.BlockSpec((1, k), lambda i, j: (0, 0)),
                  pl.BlockSpec((k, tn), lambda i, j: (0, j))],
        out_specs=[pl.BlockSpec((tm, tn), lambda i, j: (i, j)),
                   pl.BlockSpec((tm // SB_TK, 2, SB_TK, tn), lambda i, j: (i, 0, 0, j))],
        out_shape=[jax.ShapeDtypeStruct((m, n), F32), jax.ShapeDtypeStruct((m // SB_TK, 2, SB_TK, n), BF16)],
        scratch_shapes=[pltpu.VMEM((tm, k), BF16)],
        compiler_params=_params("parallel", "arbitrary"),
        name="norm_matmul_kv",
    )(x, g.reshape(1, k), w)


def _norm_mm_hi_kernel(x_ref, g_ref, w_ref, o_ref):
    hn = _rms(x_ref[...], g_ref[...])
    w = w_ref[...]
    h_hi = hn.astype(BF16)
    h_lo = (hn - h_hi.astype(F32)).astype(BF16)
    w_hi = w.astype(BF16)
    w_lo = (w - w_hi.astype(F32)).astype(BF16)
    o_ref[...] = _dot(h_hi, w_hi) + _dot(h_lo, w_hi) + _dot(h_hi, w_lo)


def _norm_matmul_hi(x, g, w, tm):
    m, k = x.shape
    n = w.shape[1]
    return pl.pallas_call(
        _norm_mm_hi_kernel,
        grid=(m // tm,),
        in_specs=[pl.BlockSpec((tm, k), lambda i: (i, 0)),
                  pl.BlockSpec((1, k), lambda i: (0, 0)),
                  pl.BlockSpec((k, n), lambda i: (0, 0))],
        out_specs=pl.BlockSpec((tm, n), lambda i: (i, 0)),
        out_shape=jax.ShapeDtypeStruct((m, n), F32),
        compiler_params=_params("parallel"),
        name="norm_matmul_hi",
    )(x, g.reshape(1, k), w)


def _sconv_prompt_kernel(b_ref, c_ref, h_ref, w_ref, y_ref, ns_ref, ubuf, *, tm):
    t = pl.program_id(1)
    pad = 8
    keep = SC_WIDTH - 1

    @pl.when(t == 0)
    def _():
        ubuf[0:pad, :] = jnp.zeros((pad, D_MODEL), F32)

    @pl.when(t > 0)
    def _():
        ubuf[0:pad, :] = ubuf[tm:tm + pad, :]

    u = c_ref[...] * h_ref[...]
    ubuf[pad:pad + tm, :] = u
    w = w_ref[...]
    conv = (ubuf[pad - 2:pad - 2 + tm, :] * w[0:1, :] + ubuf[pad - 1:pad - 1 + tm, :] * w[1:2, :]
            + u * w[2:3, :])
    y_ref[...] = (b_ref[...] * conv).astype(BF16)
    ns_ref[...] = ubuf[pad + tm - keep:pad + tm, :]


def _sconv_prompt(p1, w_conv, nb, t, tm):
    nt = t // tm
    blk = lambda c: pl.BlockSpec((tm, D_MODEL), lambda b, i, c=c: (b * nt + i, c))
    return pl.pallas_call(
        functools.partial(_sconv_prompt_kernel, tm=tm),
        grid=(nb, nt),
        in_specs=[blk(0), blk(1), blk(2), pl.BlockSpec((SC_WIDTH, D_MODEL), lambda b, i: (0, 0))],
        out_specs=[pl.BlockSpec((tm, D_MODEL), lambda b, i: (b * nt + i, 0)),
                   pl.BlockSpec((None, SC_WIDTH - 1, D_MODEL), lambda b, i: (b, 0, 0))],
        out_shape=[jax.ShapeDtypeStruct((nb * t, D_MODEL), BF16),
                   jax.ShapeDtypeStruct((nb, SC_WIDTH - 1, D_MODEL), F32)],
        scratch_shapes=[pltpu.VMEM((tm + 8, D_MODEL), F32)],
        compiler_params=_params("parallel", "arbitrary"),
        name="sconv_prompt",
    )(p1, p1, p1, w_conv)


def _ssd_epilogue(y, xs, z, dskip, gnorm):
    y = (y + dskip * xs) * _silu(z)
    outs = []
    for g in range(SSM_GROUPS):
        sl = slice(g * SSM_GROUP_DIM, (g + 1) * SSM_GROUP_DIM)
        outs.append(_rms(y[:, sl], gnorm[:, sl]))
    return jnp.concatenate(outs, axis=-1)


def _ssd_prompt_kernel(xa_ref, xb_ref, xc_ref, za_ref, zb_ref, dt_ref, cw_ref, cb_ref, dtb_ref,
                       alog_ref, dskip_ref, gn_ref, e_ref,
                       y_ref, ncs_ref, nst_ref, xbuf, st_ref, ybuf):
    c = pl.program_id(1)
    nc = pl.num_programs(1)
    L = SSM_CHUNK
    pad = 8
    keep = SSM_CONV - 1

    @pl.when(c == 0)
    def _():
        xbuf[0:pad, :] = jnp.zeros((pad, SSM_CONV_DIM), F32)
        st_ref[...] = jnp.zeros_like(st_ref)

    @pl.when(c > 0)
    def _():
        xbuf[0:pad, :] = xbuf[L:L + pad, :]

    xbuf[pad:pad + L, 0:D_MODEL] = xa_ref[...]
    xbuf[pad:pad + L, D_MODEL:2 * D_MODEL] = xb_ref[...]
    xbuf[pad:pad + L, 2 * D_MODEL:3 * D_MODEL] = xc_ref[...]
    cw = cw_ref[...]
    conv = cb_ref[...] + xbuf[pad:pad + L, :] * cw[keep:keep + 1, :]
    for j in range(keep):
        conv = conv + xbuf[pad - keep + j:pad - keep + j + L, :] * cw[j:j + 1, :]
    xbc = _silu(conv)
    ncs_ref[...] = xbuf[pad + L - keep:pad + L, :]

    xs = xbc[:, :SSM_DIM]
    bm = xbc[:, SSM_DIM:SSM_DIM + SSM_GN]
    cm = xbc[:, SSM_DIM + SSM_GN:]

    row = lax.broadcasted_iota(jnp.int32, (L, L), 0)
    col = lax.broadcasted_iota(jnp.int32, (L, L), 1)
    causal = col <= row
    tri = jnp.where(causal, 1.0, 0.0).astype(BF16)

    dt = _softplus(dt_ref[...] + dtb_ref[...])
    a = -jnp.exp(alog_ref[...])
    adt = a * dt
    h1, h2, h3 = _split3(adt)
    a_cs = _dot(tri, h1) + _dot(tri, h2) + _dot(tri, h3)
    a_cs_t = a_cs.T
    ea = jnp.exp(a_cs)
    dte = jnp.exp(a_cs[L - 1:L, :] - a_cs)
    e = e_ref[...]
    dt_x = _dot3(dt, e)
    ea_x = _dot3(ea, e)
    dte_x = _dot3(dte, e)
    xdt = xs * dt_x
    xdt_b = xdt.astype(BF16)
    xdec_b = (xdt * dte_x).astype(BF16)
    lane = lax.broadcasted_iota(jnp.int32, (L, LANES), 1)
    first_half = lane < SSM_HEAD_DIM

    for g in range(SSM_GROUPS):
        gs = slice(g * SSM_STATE, (g + 1) * SSM_STATE)
        gd = slice(g * SSM_GROUP_DIM, (g + 1) * SSM_GROUP_DIM)
        b_g = bm[:, gs]
        c_b = cm[:, gs].astype(BF16)
        cb = _dot_nt(c_b, b_g.astype(BF16))
        st_g = st_ref[g]
        ybuf[:, gd] = _dot(c_b, st_g.astype(BF16)) * ea_x[:, gd]
        for pr in range(SSM_GROUP_DIM // LANES):
            res = []
            for half in range(2):
                h = g * (SSM_HEADS // SSM_GROUPS) + 2 * pr + half
                seg = jnp.exp(jnp.where(causal, a_cs[:, h:h + 1] - a_cs_t[h:h + 1, :], -jnp.inf))
                res.append((cb * seg).astype(BF16))
            ps = slice(g * SSM_GROUP_DIM + pr * LANES, g * SSM_GROUP_DIM + (pr + 1) * LANES)
            x2 = xdt_b[:, ps]
            ybuf[:, ps] += jnp.where(first_half, _dot(res[0], x2), _dot(res[1], x2))
        st_ref[g] = st_g * ea_x[L - 1:L, gd] + _dot(b_g.T.astype(BF16), xdec_b[:, gd])

    z = jnp.concatenate([za_ref[...], zb_ref[...]], axis=-1)
    y_ref[...] = _ssd_epilogue(ybuf[...], xs, z, dskip_ref[...], gn_ref[...]).astype(BF16)

    @pl.when(c == nc - 1)
    def _():
        for g in range(SSM_GROUPS):
            nst_ref[g * SSM_GROUP_DIM:(g + 1) * SSM_GROUP_DIM, :] = st_ref[g].T


def _ssd_prompt(p1, pdt, cw, cb, dtb, alog, dskip_x, gn, e, nb, t):
    L = SSM_CHUNK
    nc = t // L
    blk = lambda c: pl.BlockSpec((L, D_MODEL), lambda b, i, c=c: (b * nc + i, c))
    full = lambda r, n: pl.BlockSpec((r, n), lambda b, i: (0, 0))
    return pl.pallas_call(
        _ssd_prompt_kernel,
        grid=(nb, nc),
        in_specs=[blk(5), blk(6), blk(7), blk(3), blk(4),
                  pl.BlockSpec((L, LANES), lambda b, i: (b * nc + i, 0)),
                  full(SSM_CONV, SSM_CONV_DIM), full(1, SSM_CONV_DIM), full(1, LANES), full(1, LANES),
                  full(1, SSM_DIM), full(1, SSM_DIM), full(LANES, SSM_DIM)],
        out_specs=[pl.BlockSpec((L, SSM_DIM), lambda b, i: (b * nc + i, 0)),
                   pl.BlockSpec((None, SSM_CONV - 1, SSM_CONV_DIM), lambda b, i: (b, 0, 0)),
                   pl.BlockSpec((None, SSM_DIM, SSM_STATE), lambda b, i: (b, 0, 0))],
        out_shape=[jax.ShapeDtypeStruct((nb * t, SSM_DIM), BF16),
                   jax.ShapeDtypeStruct((nb, SSM_CONV - 1, SSM_CONV_DIM), F32),
                   jax.ShapeDtypeStruct((nb, SSM_DIM, SSM_STATE), F32)],
        scratch_shapes=[pltpu.VMEM((L + 8, SSM_CONV_DIM), F32),
                        pltpu.VMEM((SSM_GROUPS, SSM_STATE, SSM_GROUP_DIM), F32),
                        pltpu.VMEM((L, SSM_DIM), F32)],
        compiler_params=_params("parallel", "arbitrary"),
        name="ssd_prompt",
    )(p1, p1, p1, p1, p1, pdt, cw, cb, dtb, alog, dskip_x, gn, e)


def _softplus2(z):
    neg_abs = pltpu.bitcast(pltpu.bitcast(z, jnp.int32) | jnp.int32(-2 ** 31), F32)
    return jnp.maximum(z, 0.0) + jnp.log2(1.0 + jnp.exp2(neg_abs))


def _sb_prompt_kernel(bias_ref, q_ref, k_ref, v_ref, o_ref, acc_ref, tot_ref, z_ref, w_ref):
    gq = pl.program_id(1)
    i = pl.program_id(2)
    tq, tk, grp = SB_TQ, SB_TK, SB_GROUP
    lane2 = lax.broadcasted_iota(jnp.int32, (1, 2 * tk), 1)
    r2 = lax.broadcasted_iota(jnp.int32, (2 * tk, 2 * tk), 0)
    c2 = lax.broadcasted_iota(jnp.int32, (2 * tk, 2 * tk), 1)
    same = (r2 < tk) == (c2 < tk)
    m_tot = jnp.where(same, 1.0, 0.0).astype(BF16)
    m_cum = jnp.where(same & (r2 >= c2), 1.0, 0.0).astype(BF16)
    acc_ref[...] = jnp.zeros_like(acc_ref)
    tot_ref[...] = jnp.zeros_like(tot_ref)
    q_pos = i * tq + lax.broadcasted_iota(jnp.int32, (tq, 2 * tk), 0)
    k_off = lax.broadcasted_iota(jnp.int32, (tq, 2 * tk), 1) % tk
    cols = [slice(pp * LANES, (pp + 1) * LANES) for pp in range(grp)]
    biases = [jnp.where(lane2 < tk, bias_ref[2 * (gq * grp + pp)], bias_ref[2 * (gq * grp + pp) + 1]) * LOG2E
              for pp in range(grp)]

    def logits(jb, slot):
        jb = jnp.maximum(jb, 0)
        for pp in range(grp):
            z_ref[slot, pp] = _dot_nt(q_ref[:, cols[pp]], k_ref[jb, :, :, cols[pp]].reshape(2 * tk, LANES)) + biases[pp]

    def weights(jb, slot, masked):
        if masked:
            valid = (jb * tk + k_off) < q_pos
        for pp in range(grp):
            z = z_ref[slot, pp]
            sp = _softplus2(z)
            if masked:
                sp = jnp.where(valid, sp, 0.0)
            sp_b = sp.astype(BF16)
            later = tot_ref[pp] + _dot(sp_b, m_cum)
            w = jnp.exp2(z - later)
            if masked:
                w = jnp.where(valid, w, 0.0)
            w_ref[slot, pp] = w.astype(BF16)
            tot_ref[pp] += _dot(sp_b, m_tot)

    def values(jb, slot):
        for pp in range(grp):
            acc_ref[:, cols[pp]] += _dot(w_ref[slot, pp], v_ref[jb, :, :, cols[pp]].reshape(2 * tk, LANES))

    assert tq == 2 * tk
    d0 = 2 * i + 1
    logits(d0, 0)
    logits(d0 - 1, 1)
    weights(d0, 0, True)
    logits(d0 - 2, 0)
    weights(d0 - 1, 1, True)
    values(d0, 0)

    def body(t, carry):
        a = d0 - 2 - 2 * t
        logits(a - 1, 1)
        weights(a, 0, False)
        values(a + 1, 1)
        logits(a - 2, 0)
        weights(a - 1, 1, False)
        values(a, 0)
        return carry

    lax.fori_loop(0, i, body, 0)
    values(0, 1)
    o_ref[...] = acc_ref[...].astype(BF16)


def _sb_prompt(q16, k16, v16, sb_bias, nb, t):
    nq = t // SB_TQ
    nk = t // SB_TK
    gw = SB_GROUP * LANES
    ng = D_MODEL // gw
    return pl.pallas_call(
        _sb_prompt_kernel,
        grid=(nb, ng, nq),
        in_specs=[pl.BlockSpec(memory_space=pltpu.SMEM),
                  pl.BlockSpec((SB_TQ, gw), lambda b, g, i: (b * nq + i, g)),
                  pl.BlockSpec((nk, 2, SB_TK, gw), lambda b, g, i: (b, 0, 0, g)),
                  pl.BlockSpec((nk, 2, SB_TK, gw), lambda b, g, i: (b, 0, 0, g))],
        out_specs=pl.BlockSpec((SB_TQ, gw), lambda b, g, i: (b * nq + i, g)),
        out_shape=jax.ShapeDtypeStruct((nb * t, D_MODEL), BF16),
        scratch_shapes=[pltpu.VMEM((SB_TQ, gw), F32), pltpu.VMEM((SB_GROUP, SB_TQ, 2 * SB_TK), F32),
                        pltpu.VMEM((2, SB_GROUP, SB_TQ, 2 * SB_TK), F32),
                        pltpu.VMEM((2, SB_GROUP, SB_TQ, 2 * SB_TK), BF16)],
        compiler_params=_params("parallel", "parallel", "arbitrary"),
        name="sb_prompt",
    )(sb_bias, q16, k16, v16)


def _sb_decode_kernel(pt_ref, bias_ref, q_ref, kn_ref, vn_ref, *rest, pps, n_pages):
    k_refs = rest[:pps]
    v_refs = rest[pps:2 * pps]
    o_ref, qcol_ref, tot_ref, acc_ref, new_ref = rest[2 * pps:]
    s = pl.program_id(1)
    page, nh, hd = SB_TK, SB_HEADS, SB_HEAD_DIM
    bias = bias_ref[...]
    r1 = lax.broadcasted_iota(jnp.int32, (page, page), 0)
    c1 = lax.broadcasted_iota(jnp.int32, (page, page), 1)
    m_cum = jnp.where(r1 >= c1, 1.0, 0.0).astype(BF16)
    lane = lax.broadcasted_iota(jnp.int32, (hd, LANES), 1)

    @pl.when(s == 0)
    def _():
        q = q_ref[...] * SB_SCALE
        q_t = jnp.concatenate([q, jnp.zeros((LANES - nh, hd), F32)], axis=0).T
        for h in range(nh):
            qcol_ref[h] = jnp.broadcast_to(q_t[:, h:h + 1], (hd, LANES))
        acc_ref[...] = jnp.zeros_like(acc_ref)
        past = n_pages * page
        valid = jnp.full((nh, LANES), past, jnp.int32) < jnp.full((nh, LANES), past, jnp.int32)
        z_new = jnp.sum(q * kn_ref[...], axis=-1, keepdims=True) + bias
        sp_new = jnp.where(valid, _softplus(z_new), 0.0)
        w_new = jnp.where(valid, jnp.exp(z_new - sp_new), 0.0)
        tot_ref[...] = sp_new
        new_ref[...] = w_new[:, 0:1] * vn_ref[...]

    zs, cums, sums = [], [], []
    for r in range(pps):
        rows = [jnp.sum(k_refs[r][h] * qcol_ref[h], axis=0, keepdims=True) for h in range(nh)]
        zs.append(jnp.concatenate(rows, axis=0) + bias)
    for r in range(pps):
        sp = _softplus(zs[r])
        cums.append(_dot2_right(sp, m_cum))
        sums.append(jnp.sum(sp, axis=-1, keepdims=True))
    tot = tot_ref[...]
    for r in range(pps):
        w = jnp.exp(zs[r] - (tot + cums[r]))
        tot = tot + sums[r]
        for h in range(nh):
            acc_ref[h] += v_refs[r][h] * w[h:h + 1, :]
    tot_ref[...] = tot

    @pl.when(s == pl.num_programs(1) - 1)
    def _():
        o_t = jnp.zeros((hd, LANES), F32)
        for h in range(nh):
            o_t = jnp.where(lane == h, jnp.sum(acc_ref[h], axis=-1, keepdims=True), o_t)
        o_ref[...] = o_t.T[0:nh, :] + new_ref[...]


def _sb_decode(q, k_new, v_new, sb_bias_rep, cache_k_t, cache_v_t, page_table, layer, pps):
    ns = q.shape[0]
    n_pages = page_table.shape[1]
    n_steps = n_pages // pps
    tok = pl.BlockSpec((None, SB_HEADS, SB_HEAD_DIM), lambda n, s, pt: (n, 0, 0))

    def page_spec(r):
        def imap(n, s, pt):
            return (layer, pt[n * n_pages + (n_pages - 1 - (s * pps + r))], 0, 0, 0)
        return pl.BlockSpec((None, None, SB_HEADS, SB_HEAD_DIM, SB_TK), imap)

    specs = [pl.BlockSpec((SB_HEADS, LANES), lambda n, s, pt: (0, 0)), tok, tok, tok]
    specs += [page_spec(r) for r in range(pps)] * 2
    grid_spec = pltpu.PrefetchScalarGridSpec(
        num_scalar_prefetch=1,
        grid=(ns, n_steps),
        in_specs=specs,
        out_specs=tok,
        scratch_shapes=[pltpu.VMEM((SB_HEADS, SB_HEAD_DIM, LANES), F32), pltpu.VMEM((SB_HEADS, LANES), F32),
                        pltpu.VMEM((SB_HEADS, SB_HEAD_DIM, SB_TK), F32), pltpu.VMEM((SB_HEADS, SB_HEAD_DIM), F32)])
    return pl.pallas_call(
        functools.partial(_sb_decode_kernel, pps=pps, n_pages=n_pages),
        grid_spec=grid_spec,
        out_shape=jax.ShapeDtypeStruct((ns, SB_HEADS, SB_HEAD_DIM), F32),
        compiler_params=_params("parallel", "arbitrary"),
        name="sb_decode",
    )(page_table.reshape(-1), sb_bias_rep, q, k_new, v_new, *([cache_k_t] * pps), *([cache_v_t] * pps))


def _sample_pre_kernel(p1_ref, dt_ref, ssc_ref, sxc_ref, scw_ref, cw_ref, cb_ref, dtb_ref, alog_ref, e_ref,
                       ya_ref, nsc_ref, nxc_ref, xs_ref, b_ref, c_ref, dtx_ref, dec_ref):
    d = D_MODEL
    p1 = p1_ref[...]
    u = p1[:, d:2 * d] * p1[:, 2 * d:3 * d]
    ssc = ssc_ref[...]
    scw = scw_ref[...]
    conv_a = ssc[:, 0:d] * scw[0:1, :] + ssc[:, d:2 * d] * scw[1:2, :] + u * scw[2:3, :]
    ya_ref[...] = (p1[:, 0:d] * conv_a).astype(BF16)
    nsc_ref[:, 0:d] = ssc[:, d:2 * d]
    nsc_ref[:, d:2 * d] = u
    cd = SSM_CONV_DIM
    xbc = p1[:, 3 * d + SSM_DIM:3 * d + SSM_DIM + cd]
    sxc = sxc_ref[...]
    cw = cw_ref[...]
    conv = cb_ref[...] + xbc * cw[SSM_CONV - 1:SSM_CONV, :]
    for j in range(SSM_CONV - 1):
        conv = conv + sxc[:, j * cd:(j + 1) * cd] * cw[j:j + 1, :]
        if j > 0:
            nxc_ref[:, (j - 1) * cd:j * cd] = sxc[:, j * cd:(j + 1) * cd]
    nxc_ref[:, (SSM_CONV - 2) * cd:(SSM_CONV - 1) * cd] = xbc
    act = _silu(conv)
    xs = act[:, :SSM_DIM]
    xs_ref[...] = xs
    b_ref[...] = act[:, SSM_DIM:SSM_DIM + SSM_GN]
    c_ref[...] = act[:, SSM_DIM + SSM_GN:]
    dt = _softplus(dt_ref[...] + dtb_ref[...])
    dec = jnp.exp(-jnp.exp(alog_ref[...]) * dt)
    e = e_ref[...]
    dtx_ref[...] = xs * _dot3(dt, e)
    dec_ref[...] = _dot3(dec, e)


def _sample_pre(p1s, pdts, ssc, sxc, scw, cw, cb, dtb, alog, e):
    ns = p1s.shape[0]
    f = lambda n: jax.ShapeDtypeStruct((ns, n), F32)
    return pl.pallas_call(
        _sample_pre_kernel,
        out_shape=[jax.ShapeDtypeStruct((ns, D_MODEL), BF16), f((SC_WIDTH - 1) * D_MODEL),
                   f((SSM_CONV - 1) * SSM_CONV_DIM), f(SSM_DIM), f(SSM_GN), f(SSM_GN), f(SSM_DIM), f(SSM_DIM)],
        compiler_params=pltpu.CompilerParams(vmem_limit_bytes=VMEM_LIMIT),
        name="sample_pre",
    )(p1s, pdts, ssc, sxc, scw, cw, cb, dtb, alog, e)


def _sample_state_kernel(st_ref, dtx_ref, dec_ref, b_ref, c_ref, nst_ref, y_ref):
    nblk = SSM_DIM // LANES
    zpad = jnp.zeros((LANES - nblk, LANES), F32)
    dtx_t = jnp.concatenate([dtx_ref[...], zpad], axis=0).T
    dec_t = jnp.concatenate([dec_ref[...], zpad], axis=0).T
    b = b_ref[...]
    c = c_ref[...]
    lane = lax.broadcasted_iota(jnp.int32, (LANES, LANES), 1)
    ycols = jnp.zeros((LANES, LANES), F32)
    for i in range(nblk):
        g = i // (nblk // SSM_GROUPS)
        rows = slice(i * LANES, (i + 1) * LANES)
        new = st_ref[rows, :] * dec_t[:, i:i + 1] + dtx_t[:, i:i + 1] * b[g:g + 1, :]
        nst_ref[rows, :] = new
        ycol = jnp.sum(new * c[g:g + 1, :], axis=-1, keepdims=True)
        ycols = jnp.where(lane == i, ycol, ycols)
    y_ref[...] = ycols.T[0:nblk, :]


def _sample_state(state_all, layer, dtx, dec, b, c):
    ns = state_all.shape[1]
    nblk = SSM_DIM // LANES
    vec = lambda r: pl.BlockSpec((None, r, LANES), lambda n: (n, 0, 0))
    st_spec = pl.BlockSpec((None, SSM_DIM, SSM_STATE), lambda n: (n, 0, 0))
    st_in = pl.BlockSpec((None, None, SSM_DIM, SSM_STATE), lambda n: (layer, n, 0, 0))
    nst, y = pl.pallas_call(
        _sample_state_kernel,
        grid=(ns,),
        in_specs=[st_in, vec(nblk), vec(nblk), vec(SSM_GROUPS), vec(SSM_GROUPS)],
        out_specs=[st_spec, vec(nblk)],
        out_shape=[jax.ShapeDtypeStruct((ns, SSM_DIM, SSM_STATE), F32),
                   jax.ShapeDtypeStruct((ns, nblk, LANES), F32)],
        compiler_params=_params("parallel"),
        name="sample_state",
    )(state_all, dtx.reshape(ns, nblk, LANES), dec.reshape(ns, nblk, LANES),
      b.reshape(ns, SSM_GROUPS, SSM_STATE), c.reshape(ns, SSM_GROUPS, SSM_STATE))
    return nst, y.reshape(ns, SSM_DIM)


def _sample_post_kernel(y_ref, xs_ref, p1_ref, dskip_ref, gn_ref, o_ref):
    z = p1_ref[:, 3 * D_MODEL:3 * D_MODEL + SSM_DIM]
    o_ref[...] = _ssd_epilogue(y_ref[...], xs_ref[...], z, dskip_ref[...], gn_ref[...]).astype(BF16)


def _sample_post(y, xs, p1s, dskip_x, gn):
    return pl.pallas_call(
        _sample_post_kernel,
        out_shape=jax.ShapeDtypeStruct(y.shape, BF16),
        compiler_params=pltpu.CompilerParams(vmem_limit_bytes=VMEM_LIMIT),
        name="sample_post",
    )(y, xs, p1s, dskip_x, gn)


def _merge_kernel(x_ref, ya_ref, yb_ref, yc_ref, g0_ref, g1_ref, g2_ref, wa_ref, wb_ref, wc_ref, wo_ref,
                  gx_ref, wq_ref, x1_ref, q_ref):
    merged = (jax.nn.sigmoid(g0_ref[...]) * _dot(ya_ref[...], wa_ref[...])
              + jax.nn.sigmoid(g1_ref[...]) * _dot(yb_ref[...], wb_ref[...])
              + jax.nn.sigmoid(g2_ref[...]) * _dot(yc_ref[...], wc_ref[...]))
    x1 = x_ref[...] + _dot(merged.astype(BF16), wo_ref[...])
    x1_ref[...] = x1
    hn = _rms(x1, gx_ref[...]).astype(BF16)
    q_ref[...] = (_dot(hn, wq_ref[...]) * X_SCALE).astype(BF16)


def _merge(x, ya, yb, yc, pg, wa, wb, wc, wo, gx, wq, tm):
    m = x.shape[0]
    d = D_MODEL
    rows = lambda n, c=0: pl.BlockSpec((tm, n), lambda i, c=c: (i, c))
    return pl.pallas_call(
        _merge_kernel,
        grid=(m // tm,),
        in_specs=[rows(d), rows(d), rows(SSM_DIM), rows(d), rows(d, 0), rows(d, 1), rows(d, 2),
                  _resident((d, d)), _resident((SSM_DIM, d)), _resident((d, d)), _resident((d, d)),
                  _resident((1, d)), _resident((d, d))],
        out_specs=[rows(d), rows(d)],
        out_shape=[jax.ShapeDtypeStruct((m, d), F32), jax.ShapeDtypeStruct((m, d), BF16)],
        compiler_params=_params("parallel"),
        name="merge",
    )(x, ya, yb, yc, pg, pg, pg, wa, wb, wc, wo, gx.reshape(1, d), wq)


def _mem_attn_kernel(q_ref, k_ref, v_ref, o_ref):
    q = q_ref[...]
    outs = []
    for h in range(X_HEADS):
        hs = slice(h * X_HEAD_DIM, (h + 1) * X_HEAD_DIM)
        s = _dot_nt(q[:, hs], k_ref[:, hs].astype(BF16))
        e = jnp.exp(s - jnp.max(s, axis=-1, keepdims=True))
        prob = e / jnp.sum(e, axis=-1, keepdims=True)
        outs.append(_dot(prob.astype(BF16), v_ref[:, hs].astype(BF16)))
    o_ref[...] = jnp.concatenate(outs, axis=-1).astype(BF16)


def _mem_attn(q, mem_k, mem_v, layer, tm):
    nb, t, d = q.shape
    kv = pl.BlockSpec((None, None, N_MEM, d), lambda b, i: (layer, b, 0, 0))
    qs = pl.BlockSpec((None, tm, d), lambda b, i: (b, i, 0))
    return pl.pallas_call(
        _mem_attn_kernel,
        grid=(nb, t // tm),
        in_specs=[qs, kv, kv],
        out_specs=qs,
        out_shape=jax.ShapeDtypeStruct((nb, t, d), BF16),
        compiler_params=_params("parallel", "arbitrary"),
        name="mem_attn",
    )(q, mem_k, mem_v)


def _mlp_kernel(x1_ref, at_ref, wxo_ref, gm_ref, w1_ref, w2_ref, gf_ref, o_ref, *, final):
    x2 = x1_ref[...] + _dot(at_ref[...], wxo_ref[...])
    hn = _rms(x2, gm_ref[...]).astype(BF16)
    acc = x2
    fc = 1024
    for c in range(D_FF // fc):
        h = jnp.maximum(_dot(hn, w1_ref[:, c * fc:(c + 1) * fc]), 0.0)
        acc = acc + _dot((h * h).astype(BF16), w2_ref[c * fc:(c + 1) * fc, :])
    o_ref[...] = _rms(acc, gf_ref[...]) if final else acc


def _mlp(x1, attn, wxo, gm, w1, w2, gf, tm, final):
    m, d = x1.shape
    rows = pl.BlockSpec((tm, d), lambda i: (i, 0))
    return pl.pallas_call(
        functools.partial(_mlp_kernel, final=final),
        grid=(m // tm,),
        in_specs=[rows, rows, _resident((d, d)), _resident((1, d)), _resident((d, D_FF)),
                  _resident((D_FF, d)), _resident((1, d))],
        out_specs=rows,
        out_shape=jax.ShapeDtypeStruct((m, d), F32),
        compiler_params=_params("parallel"),
        name="mlp",
    )(x1, attn, wxo, gm.reshape(1, d), w1, w2, gf.reshape(1, d))


def _pad_lanes(v):
    return jnp.pad(v.astype(F32), (0, LANES - v.shape[0])).reshape(1, LANES)


def kernel(x_prompt, x_sample, cache_sb_k, cache_sb_v, page_table, state_sconv, state_ssm_conv, state_ssm, cache_mem_k, cache_mem_v, mem_prompt, g_mix, w_in, w_sc_conv, w_ssm_conv, b_ssm_conv, dt_bias, a_log, d_skip, g_ssm_norm, sb_bias, w_br_sc, w_br_ssm, w_br_sb, w_out, g_x, g_mem, w_xq, w_xk, w_xv, w_xo, g_mlp, w_ff1, w_ff2, g_final):
    nb, t, d = x_prompt.shape
    ns = x_sample.shape[0]
    depth = w_in.shape[0]
    n_pages = page_table.shape[1]
    mp = nb * t
    tm_p = min(1024, t)
    pps = 8 if n_pages % 8 == 0 else n_pages

    xp = x_prompt.reshape(mp, d)
    xs = x_sample.reshape(ns, d)
    mem2 = mem_prompt.reshape(nb * N_MEM, d)
    st_all = state_ssm.reshape(depth, ns, SSM_DIM, SSM_STATE)
    cmk = cache_mem_k.reshape(depth, ns, N_MEM, d)
    cmv = cache_mem_v.reshape(depth, ns, N_MEM, d)
    ck_t = cache_sb_k.transpose(0, 1, 3, 4, 2)
    cv_t = cache_sb_v.transpose(0, 1, 3, 4, 2)
    head_of_channel = jnp.arange(SSM_DIM) // SSM_HEAD_DIM
    expand = (jnp.arange(LANES)[:, None] == head_of_channel[None, :]).astype(BF16)

    outs = {k: [] for k in ("p_k", "p_v", "p_sc", "p_ssc", "p_ssm", "p_mk", "p_mv",
                            "s_k", "s_v", "s_sc", "s_ssc", "s_ssm")}
    for l in range(depth):
        final = l == depth - 1
        w1 = w_in[l, :, :OFF_DT].astype(BF16)
        wqkv = w_in[l, :, OFF_Q:OFF_GATES].astype(BF16)
        wq_sb, wk_sb, wv_sb = wqkv[:, :d], wqkv[:, d:2 * d], wqkv[:, 2 * d:]
        wg = w_in[l, :, OFF_GATES:].astype(BF16)
        wdt = jnp.pad(w_in[l, :, OFF_DT:OFF_Q], ((0, 0), (0, LANES - SSM_HEADS)))
        wa, wb, wc, wo = (w[l].astype(BF16) for w in (w_br_sc, w_br_ssm, w_br_sb, w_out))
        wq, wxo, wf1, wf2 = (w[l].astype(BF16) for w in (w_xq, w_xo, w_ff1, w_ff2))
        cb = b_ssm_conv[l].reshape(1, SSM_CONV_DIM)
        dtb = _pad_lanes(dt_bias[l])
        alog = _pad_lanes(a_log[l])
        dskip_x = d_skip[l].astype(F32)[head_of_channel].reshape(1, SSM_DIM)
        gn = g_ssm_norm[l].reshape(1, SSM_DIM)
        bias_rep = jnp.broadcast_to(sb_bias[l].astype(F32)[:, None], (SB_HEADS, LANES))

        mk = _norm_matmul(mem2, g_mem[l], w_xk[l].astype(BF16), tm=min(1024, nb * N_MEM), tn=512)
        mv = _norm_matmul(mem2, g_mem[l], w_xv[l].astype(BF16), tm=min(1024, nb * N_MEM), tn=512)
        p1 = _norm_matmul(xp, g_mix[l], w1, tm=tm_p, tn=512)
        q16 = _norm_matmul(xp, g_mix[l], wq_sb, tm=tm_p, tn=512, out_dtype=BF16, out_scale=SB_SCALE * LOG2E)
        k32, k16 = _norm_matmul_kv(xp, g_mix[l], wk_sb, tm=tm_p, tn=512)
        v32, v16 = _norm_matmul_kv(xp, g_mix[l], wv_sb, tm=tm_p, tn=512)
        pg = _norm_matmul(xp, g_mix[l], wg, tm=tm_p, tn=512)
        pdt = _norm_matmul_hi(xp, g_mix[l], wdt, tm=tm_p)
        ya, p_sc = _sconv_prompt(p1, w_sc_conv[l], nb, t, tm=min(512, t))
        yb, p_ssc, p_ssm = _ssd_prompt(p1, pdt, w_ssm_conv[l], cb, dtb, alog, dskip_x, gn, expand, nb, t)
        yc = _sb_prompt(q16, k16, v16, sb_bias[l].astype(F32), nb, t)
        x1, qx = _merge(xp, ya, yb, yc, pg, wa, wb, wc, wo, g_x[l], wq, tm=256)
        attn = _mem_attn(qx.reshape(nb, t, d), mk.reshape(1, nb, N_MEM, d), mv.reshape(1, nb, N_MEM, d), 0,
                         tm=min(512, t))
        xp = _mlp(x1, attn.reshape(mp, d), wxo, g_mlp[l], wf1, wf2, g_final, tm=256, final=final)
        outs["p_k"].append(k32.reshape(nb, t, SB_HEADS, SB_HEAD_DIM))
        outs["p_v"].append(v32.reshape(nb, t, SB_HEADS, SB_HEAD_DIM))
        outs["p_sc"].append(p_sc)
        outs["p_ssc"].append(p_ssc)
        outs["p_ssm"].append(p_ssm.reshape(nb, SSM_HEADS, SSM_HEAD_DIM, SSM_STATE))
        outs["p_mk"].append(mk.reshape(nb, N_MEM, X_HEADS, X_HEAD_DIM))
        outs["p_mv"].append(mv.reshape(nb, N_MEM, X_HEADS, X_HEAD_DIM))

        p1s = _norm_matmul(xs, g_mix[l], w1, tm=ns, tn=512)
        qkv_s = _norm_matmul(xs, g_mix[l], wqkv, tm=ns, tn=512)
        pg_s = _norm_matmul(xs, g_mix[l], wg, tm=ns, tn=512)
        q_s, k_s, v_s = (qkv_s[:, c * d:(c + 1) * d].reshape(ns, SB_HEADS, SB_HEAD_DIM) for c in range(3))
        pdts = _norm_matmul_hi(xs, g_mix[l], wdt, tm=ns)
        ya_s, s_sc, s_ssc, xs_c, b_c, c_c, dtx, dec = _sample_pre(
            p1s, pdts, state_sconv[l].reshape(ns, -1), state_ssm_conv[l].reshape(ns, -1),
            w_sc_conv[l], w_ssm_conv[l], cb, dtb, alog, expand)
        s_ssm, y_s = _sample_state(st_all, l, dtx, dec, b_c, c_c)
        yb_s = _sample_post(y_s, xs_c, p1s, dskip_x, gn)
        yc_s = _sb_decode(q_s, k_s, v_s, bias_rep, ck_t, cv_t, page_table, l, pps)
        yc_s = yc_s.reshape(ns, d).astype(BF16)
        x1s, qxs = _merge(xs, ya_s, yb_s, yc_s, pg_s, wa, wb, wc, wo, g_x[l], wq, tm=ns)
        attn_s = _mem_attn(qxs.reshape(ns, 1, d), cmk, cmv, l, tm=1)
        xs = _mlp(x1s, attn_s.reshape(ns, d), wxo, g_mlp[l], wf1, wf2, g_final, tm=ns, final=final)
        outs["s_k"].append(k_s.reshape(ns, 1, SB_HEADS, SB_HEAD_DIM))
        outs["s_v"].append(v_s.reshape(ns, 1, SB_HEADS, SB_HEAD_DIM))
        outs["s_sc"].append(s_sc.reshape(ns, SC_WIDTH - 1, D_MODEL))
        outs["s_ssc"].append(s_ssc.reshape(ns, SSM_CONV - 1, SSM_CONV_DIM))
        outs["s_ssm"].append(s_ssm.reshape(ns, SSM_HEADS, SSM_HEAD_DIM, SSM_STATE))

    st = {k: jnp.stack(v) for k, v in outs.items()}
    return (xp.reshape(nb, t, d), xs.reshape(ns, 1, d),
            st["p_k"], st["p_v"], st["p_sc"], st["p_ssc"], st["p_ssm"], st["p_mk"], st["p_mv"],
            st["s_k"], st["s_v"], st["s_sc"], st["s_ssc"], st["s_ssm"])
```

```python
import functools

import jax
import jax.numpy as jnp
from jax import lax
from jax.experimental import pallas as pl
from jax.experimental.pallas import tpu as pltpu

F32 = jnp.float32
BF16 = jnp.bfloat16

D_MODEL = 1024
SC_WIDTH = 3
SSM_DIM = 2048
SSM_HEAD_DIM = 64
SSM_HEADS = 32
SSM_GROUPS = 4
SSM_STATE = 128
SSM_GN = SSM_GROUPS * SSM_STATE
SSM_GROUP_DIM = SSM_DIM // SSM_GROUPS
SSM_CONV = 4
SSM_CONV_DIM = SSM_DIM + 2 * SSM_GN
SSM_CHUNK = 128
SB_HEADS = 16
SB_HEAD_DIM = 64
SB_SCALE = SB_HEAD_DIM ** -0.5
SB_TQ = 256
SB_TK = 128
SB_GROUP = 4
LOG2E = 1.4426950408889634
X_HEADS = 4
X_HEAD_DIM = 256
X_SCALE = X_HEAD_DIM ** -0.5
N_MEM = 256
D_FF = 4096
EPS = 1e-6
LANES = 128
OFF_DT = 3 * D_MODEL + SSM_DIM + SSM_CONV_DIM
OFF_Q = OFF_DT + SSM_HEADS
OFF_GATES = OFF_Q + 3 * D_MODEL
VMEM_LIMIT = 56 * 1024 * 1024


def _params(*sem):
    return pltpu.CompilerParams(dimension_semantics=sem, vmem_limit_bytes=VMEM_LIMIT)


def _resident(shape):
    nd = len(shape)
    return pl.BlockSpec(shape, lambda *_: (0,) * nd, pipeline_mode=pl.Buffered(1))


def _rms(x, g):
    return x * lax.rsqrt(jnp.mean(x * x, axis=-1, keepdims=True) + EPS) * g


def _softplus(x):
    return jnp.maximum(x, 0.0) + jnp.log1p(jnp.exp(-jnp.abs(x)))


def _silu(x):
    return x * jax.nn.sigmoid(x)


def _split3(x):
    h1 = x.astype(BF16)
    r1 = x - h1.astype(F32)
    h2 = r1.astype(BF16)
    h3 = (r1 - h2.astype(F32)).astype(BF16)
    return h1, h2, h3


def _split2(x):
    hi = x.astype(BF16)
    return hi, (x - hi.astype(F32)).astype(BF16)


def _dot(a, b):
    return jnp.dot(a, b, preferred_element_type=F32)


def _dot_nt(a, b):
    return lax.dot_general(a, b, (((1,), (1,)), ((), ())), preferred_element_type=F32)


def _dot3(x, m):
    h1, h2, h3 = _split3(x)
    return _dot(h1, m) + _dot(h2, m) + _dot(h3, m)


def _dot2_left(m, x):
    hi, lo = _split2(x)
    return _dot(m, hi) + _dot(m, lo)


def _dot2_right(x, m):
    hi, lo = _split2(x)
    return _dot(hi, m) + _dot(lo, m)


def _norm_mm_kernel(x_ref, g_ref, w_ref, o_ref, hn_ref, *, out_scale):
    @pl.when(pl.program_id(1) == 0)
    def _():
        hn_ref[...] = _rms(x_ref[...], g_ref[...]).astype(BF16)

    r = _dot(hn_ref[...], w_ref[...])
    if out_scale is not None:
        r = r * out_scale
    o_ref[...] = r.astype(o_ref.dtype)


def _norm_matmul(x, g, w, tm, tn, out_dtype=F32, out_scale=None):
    m, k = x.shape
    n = w.shape[1]
    return pl.pallas_call(
        functools.partial(_norm_mm_kernel, out_scale=out_scale),
        grid=(m // tm, n // tn),
        in_specs=[pl.BlockSpec((tm, k), lambda i, j: (i, 0)),
                  pl.BlockSpec((1, k), lambda i, j: (0, 0)),
                  pl.BlockSpec((k, tn), lambda i, j: (0, j))],
        out_specs=pl.BlockSpec((tm, tn), lambda i, j: (i, j)),
        out_shape=jax.ShapeDtypeStruct((m, n), out_dtype),
        scratch_shapes=[pltpu.VMEM((tm, k), BF16)],
        compiler_params=_params("parallel", "arbitrary"),
        name="norm_matmul",
    )(x, g.reshape(1, k), w)


def _norm_mm_kv_kernel(x_ref, g_ref, w_ref, o32_ref, o16_ref, hn_ref, *, tm, tn):
    @pl.when(pl.program_id(1) == 0)
    def _():
        hn_ref[...] = _rms(x_ref[...], g_ref[...]).astype(BF16)

    r = _dot(hn_ref[...], w_ref[...])
    o32_ref[...] = r
    even = (lax.broadcasted_iota(jnp.int32, (1, tn), 1) // SB_HEAD_DIM) % 2 == 0
    for a in range(tm // SB_TK):
        blk = r[a * SB_TK:(a + 1) * SB_TK, :]
        o16_ref[a, 0] = jnp.where(even, blk, 0.0).astype(BF16)
        o16_ref[a, 1] = jnp.where(even, 0.0, blk).astype(BF16)


def _norm_matmul_kv(x, g, w, tm, tn):
    m, k = x.shape
    n = w.shape[1]
    return pl.pallas_call(
        functools.partial(_norm_mm_kv_kernel, tm=tm, tn=tn),
        grid=(m // tm, n // tn),
        in_specs=[pl.BlockSpec((tm, k), lambda i, j: (i, 0)),
                  pl.BlockSpec((1, k), lambda i, j: (0, 0)),
                  pl.BlockSpec((k, tn), lambda i, j: (0, j))],
        out_specs=[pl.BlockSpec((tm, tn), lambda i, j: (i, j)),
                   pl.BlockSpec((tm // SB_TK, 2, SB_TK, tn), lambda i, j: (i, 0, 0, j))],
        out_shape=[jax.ShapeDtypeStruct((m, n), F32), jax.ShapeDtypeStruct((m // SB_TK, 2, SB_TK, n), BF16)],
        scratch_shapes=[pltpu.VMEM((tm, k), BF16)],
        compiler_params=_params("parallel", "arbitrary"),
        name="norm_matmul_kv",
    )(x, g.reshape(1, k), w)


def _norm_mm_hi_kernel(x_ref, g_ref, w_ref, o_ref):
    hn = _rms(x_ref[...], g_ref[...])
    w = w_ref[...]
    h_hi = hn.astype(BF16)
    h_lo = (hn - h_hi.astype(F32)).astype(BF16)
    w_hi = w.astype(BF16)
    w_lo = (w - w_hi.astype(F32)).astype(BF16)
    o_ref[...] = _dot(h_hi, w_hi) + _dot(h_lo, w_hi) + _dot(h_hi, w_lo)


def _norm_matmul_hi(x, g, w, tm):
    m, k = x.shape
    n = w.shape[1]
    return pl.pallas_call(
        _norm_mm_hi_kernel,
        grid=(m // tm,),
        in_specs=[pl.BlockSpec((tm, k), lambda i: (i, 0)),
                  pl.BlockSpec((1, k), lambda i: (0, 0)),
                  pl.BlockSpec((k, n), lambda i: (0, 0))],
        out_specs=pl.BlockSpec((tm, n), lambda i: (i, 0)),
        out_shape=jax.ShapeDtypeStruct((m, n), F32),
        compiler_params=_params("parallel"),
        name="norm_matmul_hi",
    )(x, g.reshape(1, k), w)


def _sconv_prompt_kernel(b_ref, c_ref, h_ref, w_ref, y_ref, ns_ref, ubuf, *, tm):
    t = pl.program_id(1)
    pad = 8
    keep = SC_WIDTH - 1

    @pl.when(t == 0)
    def _():
        ubuf[0:pad, :] = jnp.zeros((pad, D_MODEL), F32)

    @pl.when(t > 0)
    def _():
        ubuf[0:pad, :] = ubuf[tm:tm + pad, :]

    u = c_ref[...] * h_ref[...]
    ubuf[pad:pad + tm, :] = u
    w = w_ref[...]
    conv = (ubuf[pad - 2:pad - 2 + tm, :] * w[0:1, :] + ubuf[pad - 1:pad - 1 + tm, :] * w[1:2, :]
            + u * w[2:3, :])
    y_ref[...] = (b_ref[...] * conv).astype(BF16)
    ns_ref[...] = ubuf[pad + tm - keep:pad + tm, :]


def _sconv_prompt(p1, w_conv, nb, t, tm):
    nt = t // tm
    blk = lambda c: pl.BlockSpec((tm, D_MODEL), lambda b, i, c=c: (b * nt + i, c))
    return pl.pallas_call(
        functools.partial(_sconv_prompt_kernel, tm=tm),
        grid=(nb, nt),
        in_specs=[blk(0), blk(1), blk(2), pl.BlockSpec((SC_WIDTH, D_MODEL), lambda b, i: (0, 0))],
        out_specs=[pl.BlockSpec((tm, D_MODEL), lambda b, i: (b * nt + i, 0)),
                   pl.BlockSpec((None, SC_WIDTH - 1, D_MODEL), lambda b, i: (b, 0, 0))],
        out_shape=[jax.ShapeDtypeStruct((nb * t, D_MODEL), BF16),
                   jax.ShapeDtypeStruct((nb, SC_WIDTH - 1, D_MODEL), F32)],
        scratch_shapes=[pltpu.VMEM((tm + 8, D_MODEL), F32)],
        compiler_params=_params("parallel", "arbitrary"),
        name="sconv_prompt",
    )(p1, p1, p1, w_conv)


def _ssd_epilogue(y, xs, z, dskip, gnorm):
    y = (y + dskip * xs) * _silu(z)
    outs = []
    for g in range(SSM_GROUPS):
        sl = slice(g * SSM_GROUP_DIM, (g + 1) * SSM_GROUP_DIM)
        outs.append(_rms(y[:, sl], gnorm[:, sl]))
    return jnp.concatenate(outs, axis=-1)


def _ssd_prompt_kernel(xa_ref, xb_ref, xc_ref, za_ref, zb_ref, dt_ref, cw_ref, cb_ref, dtb_ref,
                       alog_ref, dskip_ref, gn_ref, e_ref,
                       y_ref, ncs_ref, nst_ref, xbuf, st_ref, ybuf):
    c = pl.program_id(1)
    nc = pl.num_programs(1)
    L = SSM_CHUNK
    pad = 8
    keep = SSM_CONV - 1

    @pl.when(c == 0)
    def _():
        xbuf[0:pad, :] = jnp.zeros((pad, SSM_CONV_DIM), F32)
        st_ref[...] = jnp.zeros_like(st_ref)

    @pl.when(c > 0)
    def _():
        xbuf[0:pad, :] = xbuf[L:L + pad, :]

    xbuf[pad:pad + L, 0:D_MODEL] = xa_ref[...]
    xbuf[pad:pad + L, D_MODEL:2 * D_MODEL] = xb_ref[...]
    xbuf[pad:pad + L, 2 * D_MODEL:3 * D_MODEL] = xc_ref[...]
    cw = cw_ref[...]
    conv = cb_ref[...] + xbuf[pad:pad + L, :] * cw[keep:keep + 1, :]
    for j in range(keep):
        conv = conv + xbuf[pad - keep + j:pad - keep + j + L, :] * cw[j:j + 1, :]
    xbc = _silu(conv)
    ncs_ref[...] = xbuf[pad + L - keep:pad + L, :]

    xs = xbc[:, :SSM_DIM]
    bm = xbc[:, SSM_DIM:SSM_DIM + SSM_GN]
    cm = xbc[:, SSM_DIM + SSM_GN:]

    row = lax.broadcasted_iota(jnp.int32, (L, L), 0)
    col = lax.broadcasted_iota(jnp.int32, (L, L), 1)
    causal = col <= row
    tri = jnp.where(causal, 1.0, 0.0).astype(BF16)

    dt = _softplus(dt_ref[...] + dtb_ref[...])
    a = -jnp.exp(alog_ref[...])
    adt = a * dt
    h1, h2, h3 = _split3(adt)
    a_cs = _dot(tri, h1) + _dot(tri, h2) + _dot(tri, h3)
    a_cs_t = a_cs.T
    ea = jnp.exp(a_cs)
    dte = jnp.exp(a_cs[L - 1:L, :] - a_cs)
    e = e_ref[...]
    dt_x = _dot3(dt, e)
    ea_x = _dot3(ea, e)
    dte_x = _dot3(dte, e)
    xdt = xs * dt_x
    xdt_b = xdt.astype(BF16)
    xdec_b = (xdt * dte_x).astype(BF16)
    lane = lax.broadcasted_iota(jnp.int32, (L, LANES), 1)
    first_half = lane < SSM_HEAD_DIM

    for g in range(SSM_GROUPS):
        gs = slice(g * SSM_STATE, (g + 1) * SSM_STATE)
        gd = slice(g * SSM_GROUP_DIM, (g + 1) * SSM_GROUP_DIM)
        b_g = bm[:, gs]
        c_b = cm[:, gs].astype(BF16)
        cb = _dot_nt(c_b, b_g.astype(BF16))
        st_g = st_ref[g]
        ybuf[:, gd] = _dot(c_b, st_g.astype(BF16)) * ea_x[:, gd]
        for pr in range(SSM_GROUP_DIM // LANES):
            res = []
            for half in range(2):
                h = g * (SSM_HEADS // SSM_GROUPS) + 2 * pr + half
                seg = jnp.exp(jnp.where(causal, a_cs[:, h:h + 1] - a_cs_t[h:h + 1, :], -jnp.inf))
                res.append((cb * seg).astype(BF16))
            ps = slice(g * SSM_GROUP_DIM + pr * LANES, g * SSM_GROUP_DIM + (pr + 1) * LANES)
            x2 = xdt_b[:, ps]
            ybuf[:, ps] += jnp.where(first_half, _dot(res[0], x2), _dot(res[1], x2))
        st_ref[g] = st_g * ea_x[L - 1:L, gd] + _dot(b_g.T.astype(BF16), xdec_b[:, gd])

    z = jnp.concatenate([za_ref[...], zb_ref[...]], axis=-1)
    y_ref[...] = _ssd_epilogue(ybuf[...], xs, z, dskip_ref[...], gn_ref[...]).astype(BF16)

    @pl.when(c == nc - 1)
    def _():
        for g in range(SSM_GROUPS):
            nst_ref[g * SSM_GROUP_DIM:(g + 1) * SSM_GROUP_DIM, :] = st_ref[g].T


def _ssd_prompt(p1, pdt, cw, cb, dtb, alog, dskip_x, gn, e, nb, t):
    L = SSM_CHUNK
    nc = t // L
    blk = lambda c: pl.BlockSpec((L, D_MODEL), lambda b, i, c=c: (b * nc + i, c))
    full = lambda r, n: pl.BlockSpec((r, n), lambda b, i: (0, 0))
    return pl.pallas_call(
        _ssd_prompt_kernel,
        grid=(nb, nc),
        in_specs=[blk(5), blk(6), blk(7), blk(3), blk(4),
                  pl.BlockSpec((L, LANES), lambda b, i: (b * nc + i, 0)),
                  full(SSM_CONV, SSM_CONV_DIM), full(1, SSM_CONV_DIM), full(1, LANES), full(1, LANES),
                  full(1, SSM_DIM), full(1, SSM_DIM), full(LANES, SSM_DIM)],
        out_specs=[pl.BlockSpec((L, SSM_DIM), lambda b, i: (b * nc + i, 0)),
                   pl.BlockSpec((None, SSM_CONV - 1, SSM_CONV_DIM), lambda b, i: (b, 0, 0)),
                   pl.BlockSpec((None, SSM_DIM, SSM_STATE), lambda b, i: (b, 0, 0))],
        out_shape=[jax.ShapeDtypeStruct((nb * t, SSM_DIM), BF16),
                   jax.ShapeDtypeStruct((nb, SSM_CONV - 1, SSM_CONV_DIM), F32),
                   jax.ShapeDtypeStruct((nb, SSM_DIM, SSM_STATE), F32)],
        scratch_shapes=[pltpu.VMEM((L + 8, SSM_CONV_DIM), F32),
                        pltpu.VMEM((SSM_GROUPS, SSM_STATE, SSM_GROUP_DIM), F32),
                        pltpu.VMEM((L, SSM_DIM), F32)],
        compiler_params=_params("parallel", "arbitrary"),
        name="ssd_prompt",
    )(p1, p1, p1, p1, p1, pdt, cw, cb, dtb, alog, dskip_x, gn, e)


def _softplus2(z):
    neg_abs = pltpu.bitcast(pltpu.bitcast(z, jnp.int32) | jnp.int32(-2 ** 31), F32)
    return jnp.maximum(z, 0.0) + jnp.log2(1.0 + jnp.exp2(neg_abs))


def _sb_prompt_kernel(bias_ref, q_ref, k_ref, v_ref, o_ref, acc_ref, tot_ref, z_ref, w_ref):
    gq = pl.program_id(1)
    i = pl.program_id(2)
    tq, tk, grp = SB_TQ, SB_TK, SB_GROUP
    lane2 = lax.broadcasted_iota(jnp.int32, (1, 2 * tk), 1)
    r2 = lax.broadcasted_iota(jnp.int32, (2 * tk, 2 * tk), 0)
    c2 = lax.broadcasted_iota(jnp.int32, (2 * tk, 2 * tk), 1)
    same = (r2 < tk) == (c2 < tk)
    m_tot = jnp.where(same, 1.0, 0.0).astype(BF16)
    m_cum = jnp.where(same & (r2 >= c2), 1.0, 0.0).astype(BF16)
    acc_ref[...] = jnp.zeros_like(acc_ref)
    tot_ref[...] = jnp.zeros_like(tot_ref)
    q_pos = i * tq + lax.broadcasted_iota(jnp.int32, (tq, 2 * tk), 0)
    k_off = lax.broadcasted_iota(jnp.int32, (tq, 2 * tk), 1) % tk
    cols = [slice(pp * LANES, (pp + 1) * LANES) for pp in range(grp)]
    biases = [jnp.where(lane2 < tk, bias_ref[2 * (gq * grp + pp)], bias_ref[2 * (gq * grp + pp) + 1]) * LOG2E
              for pp in range(grp)]

    def logits(jb, slot):
        jb = jnp.maximum(jb, 0)
        for pp in range(grp):
            z_ref[slot, pp] = _dot_nt(q_ref[:, cols[pp]], k_ref[jb, :, :, cols[pp]].reshape(2 * tk, LANES)) + biases[pp]

    def weights(jb, slot, masked):
        if masked:
            valid = (jb * tk + k_off) < q_pos
        for pp in range(grp):
            z = z_ref[slot, pp]
            sp = _softplus2(z)
            if masked:
                sp = jnp.where(valid, sp, 0.0)
            sp_b = sp.astype(BF16)
            later = tot_ref[pp] + _dot(sp_b, m_cum)
            w = jnp.exp2(z - later)
            if masked:
                w = jnp.where(valid, w, 0.0)
            w_ref[slot, pp] = w.astype(BF16)
            tot_ref[pp] += _dot(sp_b, m_tot)

    def values(jb, slot):
        for pp in range(grp):
            acc_ref[:, cols[pp]] += _dot(w_ref[slot, pp], v_ref[jb, :, :, cols[pp]].reshape(2 * tk, LANES))

    assert tq == 2 * tk
    d0 = 2 * i + 1
    logits(d0, 0)
    logits(d0 - 1, 1)
    weights(d0, 0, True)
    logits(d0 - 2, 0)
    weights(d0 - 1, 1, True)
    values(d0, 0)

    def body(t, carry):
        a = d0 - 2 - 2 * t
        logits(a - 1, 1)
        weights(a, 0, False)
        values(a + 1, 1)
        logits(a - 2, 0)
        weights(a - 1, 1, False)
        values(a, 0)
        return carry

    lax.fori_loop(0, i, body, 0)
    values(0, 1)
    o_ref[...] = acc_ref[...].astype(BF16)


def _sb_prompt(q16, k16, v16, sb_bias, nb, t):
    nq = t // SB_TQ
    nk = t // SB_TK
    gw = SB_GROUP * LANES
    ng = D_MODEL // gw
    return pl.pallas_call(
        _sb_prompt_kernel,
        grid=(nb, ng, nq),
        in_specs=[pl.BlockSpec(memory_space=pltpu.SMEM),
                  pl.BlockSpec((SB_TQ, gw), lambda b, g, i: (b * nq + i, g)),
                  pl.BlockSpec((nk, 2, SB_TK, gw), lambda b, g, i: (b, 0, 0, g)),
                  pl.BlockSpec((nk, 2, SB_TK, gw), lambda b, g, i: (b, 0, 0, g))],
        out_specs=pl.BlockSpec((SB_TQ, gw), lambda b, g, i: (b * nq + i, g)),
        out_shape=jax.ShapeDtypeStruct((nb * t, D_MODEL), BF16),
        scratch_shapes=[pltpu.VMEM((SB_TQ, gw), F32), pltpu.VMEM((SB_GROUP, SB_TQ, 2 * SB_TK), F32),
                        pltpu.VMEM((2, SB_GROUP, SB_TQ, 2 * SB_TK), F32),
                        pltpu.VMEM((2, SB_GROUP, SB_TQ, 2 * SB_TK), BF16)],
        compiler_params=_params("parallel", "parallel", "arbitrary"),
        name="sb_prompt",
    )(sb_bias, q16, k16, v16)


def _sb_decode_kernel(pt_ref, bias_ref, q_ref, kn_ref, vn_ref, *rest, pps, n_pages):
    k_refs = rest[:pps]
    v_refs = rest[pps:2 * pps]
    o_ref, qcol_ref, tot_ref, acc_ref, new_ref = rest[2 * pps:]
    s = pl.program_id(1)
    page, nh, hd = SB_TK, SB_HEADS, SB_HEAD_DIM
    bias = bias_ref[...]
    r1 = lax.broadcasted_iota(jnp.int32, (page, page), 0)
    c1 = lax.broadcasted_iota(jnp.int32, (page, page), 1)
    m_cum = jnp.where(r1 >= c1, 1.0, 0.0).astype(BF16)
    lane = lax.broadcasted_iota(jnp.int32, (hd, LANES), 1)

    @pl.when(s == 0)
    def _():
        q = q_ref[...] * SB_SCALE
        q_t = jnp.concatenate([q, jnp.zeros((LANES - nh, hd), F32)], axis=0).T
        for h in range(nh):
            qcol_ref[h] = jnp.broadcast_to(q_t[:, h:h + 1], (hd, LANES))
        acc_ref[...] = jnp.zeros_like(acc_ref)
        past = n_pages * page
        valid = jnp.full((nh, LANES), past, jnp.int32) < jnp.full((nh, LANES), past, jnp.int32)
        z_new = jnp.sum(q * kn_ref[...], axis=-1, keepdims=True) + bias
        sp_new = jnp.where(valid, _softplus(z_new), 0.0)
        w_new = jnp.where(valid, jnp.exp(z_new - sp_new), 0.0)
        tot_ref[...] = sp_new
        new_ref[...] = w_new[:, 0:1] * vn_ref[...]

    zs, cums, sums = [], [], []
    for r in range(pps):
        rows = [jnp.sum(k_refs[r][h] * qcol_ref[h], axis=0, keepdims=True) for h in range(nh)]
        zs.append(jnp.concatenate(rows, axis=0) + bias)
    for r in range(pps):
        sp = _softplus(zs[r])
        cums.append(_dot2_right(sp, m_cum))
        sums.append(jnp.sum(sp, axis=-1, keepdims=True))
    tot = tot_ref[...]
    for r in range(pps):
        w = jnp.exp(zs[r] - (tot + cums[r]))
        tot = tot + sums[r]
        for h in range(nh):
            acc_ref[h] += v_refs[r][h] * w[h:h + 1, :]
    tot_ref[...] = tot

    @pl.when(s == pl.num_programs(1) - 1)
    def _():
        o_t = jnp.zeros((hd, LANES), F32)
        for h in range(nh):
            o_t = jnp.where(lane == h, jnp.sum(acc_ref[h], axis=-1, keepdims=True), o_t)
        o_ref[...] = o_t.T[0:nh, :] + new_ref[...]


def _sb_decode(q, k_new, v_new, sb_bias_rep, cache_k_t, cache_v_t, page_table, layer, pps):
    ns = q.shape[0]
    n_pages = page_table.shape[1]
    n_steps = n_pages // pps
    tok = pl.BlockSpec((None, SB_HEADS, SB_HEAD_DIM), lambda n, s, pt: (n, 0, 0))

    def page_spec(r):
        def imap(n, s, pt):
            return (layer, pt[n * n_pages + (n_pages - 1 - (s * pps + r))], 0, 0, 0)
        return pl.BlockSpec((None, None, SB_HEADS, SB_HEAD_DIM, SB_TK), imap)

    specs = [pl.BlockSpec((SB_HEADS, LANES), lambda n, s, pt: (0, 0)), tok, tok, tok]
    specs += [page_spec(r) for r in range(pps)] * 2
    grid_spec = pltpu.PrefetchScalarGridSpec(
        num_scalar_prefetch=1,
        grid=(ns, n_steps),
        in_specs=specs,
        out_specs=tok,
        scratch_shapes=[pltpu.VMEM((SB_HEADS, SB_HEAD_DIM, LANES), F32), pltpu.VMEM((SB_HEADS, LANES), F32),
                        pltpu.VMEM((SB_HEADS, SB_HEAD_DIM, SB_TK), F32), pltpu.VMEM((SB_HEADS, SB_HEAD_DIM), F32)])
    return pl.pallas_call(
        functools.partial(_sb_decode_kernel, pps=pps, n_pages=n_pages),
        grid_spec=grid_spec,
        out_shape=jax.ShapeDtypeStruct((ns, SB_HEADS, SB_HEAD_DIM), F32),
        compiler_params=_params("parallel", "arbitrary"),
        name="sb_decode",
    )(page_table.reshape(-1), sb_bias_rep, q, k_new, v_new, *([cache_k_t] * pps), *([cache_v_t] * pps))


def _sample_pre_kernel(p1_ref, dt_ref, ssc_ref, sxc_ref, scw_ref, cw_ref, cb_ref, dtb_ref, alog_ref, e_ref,
                       ya_ref, nsc_ref, nxc_ref, xs_ref, b_ref, c_ref, dtx_ref, dec_ref):
    d = D_MODEL
    p1 = p1_ref[...]
    u = p1[:, d:2 * d] * p1[:, 2 * d:3 * d]
    ssc = ssc_ref[...]
    scw = scw_ref[...]
    conv_a = ssc[:, 0:d] * scw[0:1, :] + ssc[:, d:2 * d] * scw[1:2, :] + u * scw[2:3, :]
    ya_ref[...] = (p1[:, 0:d] * conv_a).astype(BF16)
    nsc_ref[:, 0:d] = ssc[:, d:2 * d]
    nsc_ref[:, d:2 * d] = u
    cd = SSM_CONV_DIM
    xbc = p1[:, 3 * d + SSM_DIM:3 * d + SSM_DIM + cd]
    sxc = sxc_ref[...]
    cw = cw_ref[...]
    conv = cb_ref[...] + xbc * cw[SSM_CONV - 1:SSM_CONV, :]
    for j in range(SSM_CONV - 1):
        conv = conv + sxc[:, j * cd:(j + 1) * cd] * cw[j:j + 1, :]
        if j > 0:
            nxc_ref[:, (j - 1) * cd:j * cd] = sxc[:, j * cd:(j + 1) * cd]
    nxc_ref[:, (SSM_CONV - 2) * cd:(SSM_CONV - 1) * cd] = xbc
    act = _silu(conv)
    xs = act[:, :SSM_DIM]
    xs_ref[...] = xs
    b_ref[...] = act[:, SSM_DIM:SSM_DIM + SSM_GN]
    c_ref[...] = act[:, SSM_DIM + SSM_GN:]
    dt = _softplus(dt_ref[...] + dtb_ref[...])
    dec = jnp.exp(-jnp.exp(alog_ref[...]) * dt)
    e = e_ref[...]
    dtx_ref[...] = xs * _dot3(dt, e)
    dec_ref[...] = _dot3(dec, e)


def _sample_pre(p1s, pdts, ssc, sxc, scw, cw, cb, dtb, alog, e):
    ns = p1s.shape[0]
    f = lambda n: jax.ShapeDtypeStruct((ns, n), F32)
    return pl.pallas_call(
        _sample_pre_kernel,
        out_shape=[jax.ShapeDtypeStruct((ns, D_MODEL), BF16), f((SC_WIDTH - 1) * D_MODEL),
                   f((SSM_CONV - 1) * SSM_CONV_DIM), f(SSM_DIM), f(SSM_GN), f(SSM_GN), f(SSM_DIM), f(SSM_DIM)],
        compiler_params=pltpu.CompilerParams(vmem_limit_bytes=VMEM_LIMIT),
        name="sample_pre",
    )(p1s, pdts, ssc, sxc, scw, cw, cb, dtb, alog, e)


def _sample_state_kernel(st_ref, dtx_ref, dec_ref, b_ref, c_ref, nst_ref, y_ref):
    nblk = SSM_DIM // LANES
    zpad = jnp.zeros((LANES - nblk, LANES), F32)
    dtx_t = jnp.concatenate([dtx_ref[...], zpad], axis=0).T
    dec_t = jnp.concatenate([dec_ref[...], zpad], axis=0).T
    b = b_ref[...]
    c_hi, c_lo = _split2(jnp.concatenate([c_ref[...], jnp.zeros((nblk - SSM_GROUPS, LANES), F32)], axis=0))
    yrows = []
    for i in range(nblk):
        g = i // (nblk // SSM_GROUPS)
        rows = slice(i * LANES, (i + 1) * LANES)
        new = st_ref[rows, :] * dec_t[:, i:i + 1] + dtx_t[:, i:i + 1] * b[g:g + 1, :]
        nst_ref[rows, :] = new
        n_hi, n_lo = _split2(new)
        y_all = _dot_nt(c_hi, n_hi) + _dot_nt(c_lo, n_hi) + _dot_nt(c_hi, n_lo)
        yrows.append(y_all[g:g + 1, :])
    y_ref[...] = jnp.concatenate(yrows, axis=0)


def _sample_state(state_all, layer, dtx, dec, b, c):
    ns = state_all.shape[1]
    nblk = SSM_DIM // LANES
    vec = lambda r: pl.BlockSpec((None, r, LANES), lambda n: (n, 0, 0))
    st_spec = pl.BlockSpec((None, SSM_DIM, SSM_STATE), lambda n: (n, 0, 0))
    st_in = pl.BlockSpec((None, None, SSM_DIM, SSM_STATE), lambda n: (layer, n, 0, 0))
    nst, y = pl.pallas_call(
        _sample_state_kernel,
        grid=(ns,),
        in_specs=[st_in, vec(nblk), vec(nblk), vec(SSM_GROUPS), vec(SSM_GROUPS)],
        out_specs=[st_spec, vec(nblk)],
        out_shape=[jax.ShapeDtypeStruct((ns, SSM_DIM, SSM_STATE), F32),
                   jax.ShapeDtypeStruct((ns, nblk, LANES), F32)],
        compiler_params=_params("parallel"),
        name="sample_state",
    )(state_all, dtx.reshape(ns, nblk, LANES), dec.reshape(ns, nblk, LANES),
      b.reshape(ns, SSM_GROUPS, SSM_STATE), c.reshape(ns, SSM_GROUPS, SSM_STATE))
    return nst, y.reshape(ns, SSM_DIM)


def _sample_post_kernel(y_ref, xs_ref, p1_ref, dskip_ref, gn_ref, o_ref):
    z = p1_ref[:, 3 * D_MODEL:3 * D_MODEL + SSM_DIM]
    o_ref[...] = _ssd_epilogue(y_ref[...], xs_ref[...], z, dskip_ref[...], gn_ref[...]).astype(BF16)


def _sample_post(y, xs, p1s, dskip_x, gn):
    return pl.pallas_call(
        _sample_post_kernel,
        out_shape=jax.ShapeDtypeStruct(y.shape, BF16),
        compiler_params=pltpu.CompilerParams(vmem_limit_bytes=VMEM_LIMIT),
        name="sample_post",
    )(y, xs, p1s, dskip_x, gn)


def _merge_kernel(x_ref, ya_ref, yb_ref, yc_ref, g0_ref, g1_ref, g2_ref, wa_ref, wb_ref, wc_ref, wo_ref,
                  gx_ref, wq_ref, x1_ref, q_ref):
    merged = (jax.nn.sigmoid(g0_ref[...]) * _dot(ya_ref[...], wa_ref[...])
              + jax.nn.sigmoid(g1_ref[...]) * _dot(yb_ref[...], wb_ref[...])
              + jax.nn.sigmoid(g2_ref[...]) * _dot(yc_ref[...], wc_ref[...]))
    x1 = x_ref[...] + _dot(merged.astype(BF16), wo_ref[...])
    x1_ref[...] = x1
    hn = _rms(x1, gx_ref[...]).astype(BF16)
    q_ref[...] = (_dot(hn, wq_ref[...]) * X_SCALE).astype(BF16)


def _merge(x, ya, yb, yc, pg, wa, wb, wc, wo, gx, wq, tm):
    m = x.shape[0]
    d = D_MODEL
    rows = lambda n, c=0: pl.BlockSpec((tm, n), lambda i, c=c: (i, c))
    return pl.pallas_call(
        _merge_kernel,
        grid=(m // tm,),
        in_specs=[rows(d), rows(d), rows(SSM_DIM), rows(d), rows(d, 0), rows(d, 1), rows(d, 2),
                  _resident((d, d)), _resident((SSM_DIM, d)), _resident((d, d)), _resident((d, d)),
                  _resident((1, d)), _resident((d, d))],
        out_specs=[rows(d), rows(d)],
        out_shape=[jax.ShapeDtypeStruct((m, d), F32), jax.ShapeDtypeStruct((m, d), BF16)],
        compiler_params=_params("parallel"),
        name="merge",
    )(x, ya, yb, yc, pg, pg, pg, wa, wb, wc, wo, gx.reshape(1, d), wq)


def _mem_attn_kernel(q_ref, k_ref, v_ref, o_ref):
    q = q_ref[...]
    outs = []
    for h in range(X_HEADS):
        hs = slice(h * X_HEAD_DIM, (h + 1) * X_HEAD_DIM)
        s = _dot_nt(q[:, hs], k_ref[:, hs].astype(BF16))
        e = jnp.exp(s - jnp.max(s, axis=-1, keepdims=True))
        prob = e / jnp.sum(e, axis=-1, keepdims=True)
        outs.append(_dot(prob.astype(BF16), v_ref[:, hs].astype(BF16)))
    o_ref[...] = jnp.concatenate(outs, axis=-1).astype(BF16)


def _mem_attn(q, mem_k, mem_v, layer, tm):
    nb, t, d = q.shape
    kv = pl.BlockSpec((None, None, N_MEM, d), lambda b, i: (layer, b, 0, 0))
    qs = pl.BlockSpec((None, tm, d), lambda b, i: (b, i, 0))
    return pl.pallas_call(
        _mem_attn_kernel,
        grid=(nb, t // tm),
        in_specs=[qs, kv, kv],
        out_specs=qs,
        out_shape=jax.ShapeDtypeStruct((nb, t, d), BF16),
        compiler_params=_params("parallel", "arbitrary"),
        name="mem_attn",
    )(q, mem_k, mem_v)


def _mlp_kernel(x1_ref, at_ref, wxo_ref, gm_ref, w1_ref, w2_ref, gf_ref, o_ref, *, final):
    x2 = x1_ref[...] + _dot(at_ref[...], wxo_ref[...])
    hn = _rms(x2, gm_ref[...]).astype(BF16)
    acc = x2
    fc = 1024
    for c in range(D_FF // fc):
        h = jnp.maximum(_dot(hn, w1_ref[:, c * fc:(c + 1) * fc]), 0.0)
        acc = acc + _dot((h * h).astype(BF16), w2_ref[c * fc:(c + 1) * fc, :])
    o_ref[...] = _rms(acc, gf_ref[...]) if final else acc


def _mlp(x1, attn, wxo, gm, w1, w2, gf, tm, final):
    m, d = x1.shape
    rows = pl.BlockSpec((tm, d), lambda i: (i, 0))
    return pl.pallas_call(
        functools.partial(_mlp_kernel, final=final),
        grid=(m // tm,),
        in_specs=[rows, rows, _resident((d, d)), _resident((1, d)), _resident((d, D_FF)),
                  _resident((D_FF, d)), _resident((1, d))],
        out_specs=rows,
        out_shape=jax.ShapeDtypeStruct((m, d), F32),
        compiler_params=_params("parallel"),
        name="mlp",
    )(x1, attn, wxo, gm.reshape(1, d), w1, w2, gf.reshape(1, d))


def _pad_lanes(v):
    return jnp.pad(v.astype(F32), (0, LANES - v.shape[0])).reshape(1, LANES)


def kernel(x_prompt, x_sample, cache_sb_k, cache_sb_v, page_table, state_sconv, state_ssm_conv, state_ssm, cache_mem_k, cache_mem_v, mem_prompt, g_mix, w_in, w_sc_conv, w_ssm_conv, b_ssm_conv, dt_bias, a_log, d_skip, g_ssm_norm, sb_bias, w_br_sc, w_br_ssm, w_br_sb, w_out, g_x, g_mem, w_xq, w_xk, w_xv, w_xo, g_mlp, w_ff1, w_ff2, g_final):
    nb, t, d = x_prompt.shape
    ns = x_sample.shape[0]
    depth = w_in.shape[0]
    n_pages = page_table.shape[1]
    mp = nb * t
    tm_p = min(2048, t)
    pps = 8 if n_pages % 8 == 0 else n_pages

    xp = x_prompt.reshape(mp, d)
    xs = x_sample.reshape(ns, d)
    mem2 = mem_prompt.reshape(nb * N_MEM, d)
    st_all = state_ssm.reshape(depth, ns, SSM_DIM, SSM_STATE)
    cmk = cache_mem_k.reshape(depth, ns, N_MEM, d)
    cmv = cache_mem_v.reshape(depth, ns, N_MEM, d)
    ck_t = cache_sb_k.transpose(0, 1, 3, 4, 2)
    cv_t = cache_sb_v.transpose(0, 1, 3, 4, 2)
    head_of_channel = jnp.arange(SSM_DIM) // SSM_HEAD_DIM
    expand = (jnp.arange(LANES)[:, None] == head_of_channel[None, :]).astype(BF16)

    outs = {k: [] for k in ("p_k", "p_v", "p_sc", "p_ssc", "p_ssm", "p_mk", "p_mv",
                            "s_k", "s_v", "s_sc", "s_ssc", "s_ssm")}
    for l in range(depth):
        final = l == depth - 1
        w1 = w_in[l, :, :OFF_DT].astype(BF16)
        wqkv = w_in[l, :, OFF_Q:OFF_GATES].astype(BF16)
        wq_sb, wk_sb, wv_sb = wqkv[:, :d], wqkv[:, d:2 * d], wqkv[:, 2 * d:]
        wg = w_in[l, :, OFF_GATES:].astype(BF16)
        wdt = jnp.pad(w_in[l, :, OFF_DT:OFF_Q], ((0, 0), (0, LANES - SSM_HEADS)))
        wa, wb, wc, wo = (w[l].astype(BF16) for w in (w_br_sc, w_br_ssm, w_br_sb, w_out))
        wq, wxo, wf1, wf2 = (w[l].astype(BF16) for w in (w_xq, w_xo, w_ff1, w_ff2))
        cb = b_ssm_conv[l].reshape(1, SSM_CONV_DIM)
        dtb = _pad_lanes(dt_bias[l])
        alog = _pad_lanes(a_log[l])
        dskip_x = d_skip[l].astype(F32)[head_of_channel].reshape(1, SSM_DIM)
        gn = g_ssm_norm[l].reshape(1, SSM_DIM)
        bias_rep = jnp.broadcast_to(sb_bias[l].astype(F32)[:, None], (SB_HEADS, LANES))

        mk = _norm_matmul(mem2, g_mem[l], w_xk[l].astype(BF16), tm=min(1024, nb * N_MEM), tn=512)
        mv = _norm_matmul(mem2, g_mem[l], w_xv[l].astype(BF16), tm=min(1024, nb * N_MEM), tn=512)
        p1 = _norm_matmul(xp, g_mix[l], w1, tm=tm_p, tn=512)
        q16 = _norm_matmul(xp, g_mix[l], wq_sb, tm=tm_p, tn=512, out_dtype=BF16, out_scale=SB_SCALE * LOG2E)
        k32, k16 = _norm_matmul_kv(xp, g_mix[l], wk_sb, tm=tm_p, tn=512)
        v32, v16 = _norm_matmul_kv(xp, g_mix[l], wv_sb, tm=tm_p, tn=512)
        pg = _norm_matmul(xp, g_mix[l], wg, tm=tm_p, tn=512)
        pdt = _norm_matmul_hi(xp, g_mix[l], wdt, tm=tm_p)
        ya, p_sc = _sconv_prompt(p1, w_sc_conv[l], nb, t, tm=min(512, t))
        yb, p_ssc, p_ssm = _ssd_prompt(p1, pdt, w_ssm_conv[l], cb, dtb, alog, dskip_x, gn, expand, nb, t)
        yc = _sb_prompt(q16, k16, v16, sb_bias[l].astype(F32), nb, t)
        x1, qx = _merge(xp, ya, yb, yc, pg, wa, wb, wc, wo, g_x[l], wq, tm=256)
        attn = _mem_attn(qx.reshape(nb, t, d), mk.reshape(1, nb, N_MEM, d), mv.reshape(1, nb, N_MEM, d), 0,
                         tm=min(512, t))
        xp = _mlp(x1, attn.reshape(mp, d), wxo, g_mlp[l], wf1, wf2, g_final, tm=256, final=final)
        outs["p_k"].append(k32.reshape(nb, t, SB_HEADS, SB_HEAD_DIM))
        outs["p_v"].append(v32.reshape(nb, t, SB_HEADS, SB_HEAD_DIM))
        outs["p_sc"].append(p_sc)
        outs["p_ssc"].append(p_ssc)
        outs["p_ssm"].append(p_ssm.reshape(nb, SSM_HEADS, SSM_HEAD_DIM, SSM_STATE))
        outs["p_mk"].append(mk.reshape(nb, N_MEM, X_HEADS, X_HEAD_DIM))
        outs["p_mv"].append(mv.reshape(nb, N_MEM, X_HEADS, X_HEAD_DIM))

        p1s = _norm_matmul(xs, g_mix[l], w1, tm=ns, tn=512)
        qkv_s = _norm_matmul(xs, g_mix[l], wqkv, tm=ns, tn=512)
        pg_s = _norm_matmul(xs, g_mix[l], wg, tm=ns, tn=512)
        q_s, k_s, v_s = (qkv_s[:, c * d:(c + 1) * d].reshape(ns, SB_HEADS, SB_HEAD_DIM) for c in range(3))
        pdts = _norm_matmul_hi(xs, g_mix[l], wdt, tm=ns)
        ya_s, s_sc, s_ssc, xs_c, b_c, c_c, dtx, dec = _sample_pre(
            p1s, pdts, state_sconv[l].reshape(ns, -1), state_ssm_conv[l].reshape(ns, -1),
            w_sc_conv[l], w_ssm_conv[l], cb, dtb, alog, expand)
        s_ssm, y_s = _sample_state(st_all, l, dtx, dec, b_c, c_c)
        yb_s = _sample_post(y_s, xs_c, p1s, dskip_x, gn)
        yc_s = _sb_decode(q_s, k_s, v_s, bias_rep, ck_t, cv_t, page_table, l, pps)
        yc_s = yc_s.reshape(ns, d).astype(BF16)
        x1s, qxs = _merge(xs, ya_s, yb_s, yc_s, pg_s, wa, wb, wc, wo, g_x[l], wq, tm=ns)
        attn_s = _mem_attn(qxs.reshape(ns, 1, d), cmk, cmv, l, tm=1)
        xs = _mlp(x1s, attn_s.reshape(ns, d), wxo, g_mlp[l], wf1, wf2, g_final, tm=ns, final=final)
        outs["s_k"].append(k_s.reshape(ns, 1, SB_HEADS, SB_HEAD_DIM))
        outs["s_v"].append(v_s.reshape(ns, 1, SB_HEADS, SB_HEAD_DIM))
        outs["s_sc"].append(s_sc.reshape(ns, SC_WIDTH - 1, D_MODEL))
        outs["s_ssc"].append(s_ssc.reshape(ns, SSM_CONV - 1, SSM_CONV_DIM))
        outs["s_ssm"].append(s_ssm.reshape(ns, SSM_HEADS, SSM_HEAD_DIM, SSM_STATE))

    st = {k: jnp.stack(v) for k, v in outs.items()}
    return (xp.reshape(nb, t, d), xs.reshape(ns, 1, d),
            st["p_k"], st["p_v"], st["p_sc"], st["p_ssc"], st["p_ssm"], st["p_mk"], st["p_mv"],
            st["s_k"], st["s_v"], st["s_sc"], st["s_ssc"], st["s_ssm"])
```
